```python
import math
import jax, jax.numpy as jnp
from jax import lax
import numpy as np

D_MODEL = 1024
BATCH = 2
SEQ = 16384
DEPTH = 4

N_MIXERS = 3
N_META = 16
BLOCK_Q = 128
RMS_EPS = 1e-6
SB_HEADS = 8
SB_HEAD_DIM = D_MODEL // SB_HEADS
SB_SCALE = 1.0 / math.sqrt(SB_HEAD_DIM)
MLA_HEADS = 8
MLA_NOPE_DIM = 128
MLA_ROPE_DIM = 64
MLA_V_DIM = 128
MLA_Q_RANK = 3 * D_MODEL // 8
MLA_KV_RANK = D_MODEL // 4
MLA_SCALE = 1.0 / math.sqrt(MLA_NOPE_DIM + MLA_ROPE_DIM)
ROPE_THETA = 10000.0
RW_HEAD_DIM = 64
RW_HEADS = D_MODEL // RW_HEAD_DIM
RW_DECAY_LORA = 64
RW_A_LORA = 64
RW_GATE_LORA = 160
RW_LN_EPS = 64e-5
D_FF = ((8 * D_MODEL // 3 + 127) // 128) * 128
N_SB = (DEPTH + 2) // 3
N_MLA = (DEPTH + 1) // 3
N_RWKV = DEPTH // 3

kernel_name = 'hybrid_sb_mla_rwkv7_macaron'


def _rmsnorm(x, g):
    xf = x.astype(jnp.float32)
    y = xf * lax.rsqrt(jnp.mean(xf * xf, axis=-1, keepdims=True) + RMS_EPS)
    return (y * g.astype(jnp.float32)).astype(x.dtype)


def _swiglu(h, w_in, w_out):
    gate, up = jnp.split(h @ w_in, 2, axis=-1)
    return (jax.nn.silu(gate) * up) @ w_out


def _query_blocks(length):
    blocks = [(0, N_META)]
    for q0 in range(N_META, length, BLOCK_Q):
        blocks.append((q0, min(BLOCK_Q, length - q0)))
    return blocks


def _block_sweep(q, k, v, block_fn):
    outs = []
    for q0, tq in _query_blocks(q.shape[2]):
        tk = q0 + tq
        outs.append(block_fn(q[:, :, q0:q0 + tq], k[:, :, :tk], v[:, :, :tk], q0))
    return jnp.concatenate(outs, axis=2)


def _stick_breaking_block(qb, kp, vp, q0):
    tq, tk = qb.shape[2], kp.shape[2]
    nkb = -(-tk // BLOCK_Q)
    pad = nkb * BLOCK_Q - tk
    kp = jnp.pad(kp, ((0, 0), (0, 0), (0, pad), (0, 0)))
    vp = jnp.pad(vp, ((0, 0), (0, 0), (0, pad), (0, 0)))
    z = jnp.einsum('bhqd,bhkd->bhqk', qb, kp).astype(jnp.float32)
    valid = jnp.arange(nkb * BLOCK_Q)[None, :] < (q0 + jnp.arange(tq))[:, None]
    log_1m = jnp.where(valid, -jax.nn.softplus(z), 0.0)
    lb = log_1m.reshape(log_1m.shape[0], log_1m.shape[1], tq, nkb, BLOCK_Q)
    idx = jnp.arange(BLOCK_Q)
    incl = (idx[:, None] >= idx[None, :]).astype(jnp.float32)
    within = jnp.einsum('bhqnk,kj->bhqnj', lb, incl)
    bidx = jnp.arange(nkb)
    later = (bidx[:, None] > bidx[None, :]).astype(jnp.float32)
    carry = jnp.einsum('bhqn,nm->bhqm', within[..., 0], later)
    r = (within + carry[..., None]).reshape(z.shape)
    a = jnp.where(valid, jnp.exp(z + r), 0.0)
    return jnp.einsum('bhqk,bhkd->bhqd', a.astype(vp.dtype), vp)


def _softmax_block(qb, kp, vp, q0):
    tq, tk = qb.shape[2], kp.shape[2]
    z = jnp.einsum('bhqd,bhkd->bhqk', qb, kp).astype(jnp.float32) * MLA_SCALE
    valid = jnp.arange(tk)[None, :] <= (q0 + jnp.arange(tq))[:, None]
    p = jax.nn.softmax(jnp.where(valid, z, -jnp.inf), axis=-1)
    return jnp.einsum('bhqk,bhkd->bhqd', p.astype(vp.dtype), vp)


def _stick_breaking_mixer(h, w_in, w_out):
    b, l, _ = h.shape
    qkv = (h @ w_in).reshape(b, l, 3, SB_HEADS, SB_HEAD_DIM)
    qkv = jnp.transpose(qkv, (2, 0, 3, 1, 4))
    q = qkv[0] * jnp.asarray(SB_SCALE, qkv.dtype)
    o = _block_sweep(q, qkv[1], qkv[2], _stick_breaking_block)
    return jnp.transpose(o, (0, 2, 1, 3)).reshape(b, l, SB_HEADS * SB_HEAD_DIM) @ w_out


def _rope(t, pos):
    half = t.shape[-1] // 2
    inv_freq = ROPE_THETA ** (-jnp.arange(half, dtype=jnp.float32) / half)
    ang = pos.astype(jnp.float32)[:, None] * inv_freq[None, :]
    cos = jnp.cos(ang)[None, :, None, :]
    sin = jnp.sin(ang)[None, :, None, :]
    tf = t.astype(jnp.float32)
    t1, t2 = tf[..., :half], tf[..., half:]
    return jnp.concatenate([t1 * cos - t2 * sin, t1 * sin + t2 * cos], axis=-1).astype(t.dtype)


def _mla_mixer(h, w_in, q_norm, kv_norm, w_uq, w_ukv, w_out):
    b, l, _ = h.shape
    c = h @ w_in
    c_q = c[..., :MLA_Q_RANK]
    c_kv = c[..., MLA_Q_RANK:MLA_Q_RANK + MLA_KV_RANK]
    k_rope = c[..., MLA_Q_RANK + MLA_KV_RANK:]
    pos = jnp.arange(l)
    q = (_rmsnorm(c_q, q_norm) @ w_uq).reshape(b, l, MLA_HEADS, MLA_NOPE_DIM + MLA_ROPE_DIM)
    kv = (_rmsnorm(c_kv, kv_norm) @ w_ukv).reshape(b, l, MLA_HEADS, MLA_NOPE_DIM + MLA_V_DIM)
    q = jnp.concatenate([q[..., :MLA_NOPE_DIM], _rope(q[..., MLA_NOPE_DIM:], pos)], axis=-1)
    k_rope = jnp.broadcast_to(_rope(k_rope[:, :, None, :], pos), (b, l, MLA_HEADS, MLA_ROPE_DIM))
    k = jnp.concatenate([kv[..., :MLA_NOPE_DIM], k_rope], axis=-1)
    v = kv[..., MLA_NOPE_DIM:]
    o = _block_sweep(jnp.transpose(q, (0, 2, 1, 3)), jnp.transpose(k, (0, 2, 1, 3)),
                     jnp.transpose(v, (0, 2, 1, 3)), _softmax_block)
    return jnp.transpose(o, (0, 2, 1, 3)).reshape(b, l, MLA_HEADS * MLA_V_DIM) @ w_out


def _rwkv7_scan(r, w, k, v, a, bb):
    nb, _, nh, n = r.shape

    def step(s, inp):
        r_t, w_t, k_t, v_t, a_t, b_t = inp
        sa = jnp.einsum('bhij,bhj->bhi', s, a_t)
        s = s * w_t[:, :, None, :] + sa[..., None] * b_t[:, :, None, :] + v_t[..., None] * k_t[:, :, None, :]
        return s, jnp.einsum('bhij,bhj->bhi', s, r_t)

    xs = tuple(jnp.swapaxes(t, 0, 1) for t in (r, w, k, v, a, bb))
    s0 = jnp.zeros((nb, nh, n, n), jnp.float32)
    _, y = lax.scan(step, s0, xs)
    return jnp.swapaxes(y, 0, 1)


def _rwkv7_mixer(h, mix, w_in, w0, w1, w2, a0, a1, a2, g1, g2, k_k, k_a, r_k, ln_w, ln_b, w_out):
    b, l, d = h.shape
    f32 = jnp.float32
    heads = lambda t: t.reshape(b, l, RW_HEADS, RW_HEAD_DIM)
    xx = jnp.pad(h, ((0, 0), (1, 0), (0, 0)))[:, :-1] - h
    xs = h[None] + xx[None] * mix[:, None, None, :]
    rkv = jnp.einsum('nbld,nde->nble', xs[:3], w_in)
    r, k, v = rkv[0].astype(f32), rkv[1].astype(f32), rkv[2].astype(f32)
    xw, xa, xg = xs[3], xs[4], xs[5]
    w_log = -jax.nn.softplus(-(w0 + jnp.tanh(xw @ w1) @ w2).astype(f32)) - 0.5
    decay = jnp.exp(-jnp.exp(w_log))
    a = jax.nn.sigmoid((a0 + (xa @ a1) @ a2).astype(f32))
    g = (jax.nn.sigmoid(xg @ g1) @ g2).astype(f32)
    kk = heads(k * k_k.astype(f32))
    kk = kk / jnp.maximum(jnp.linalg.norm(kk, axis=-1, keepdims=True), 1e-12)
    k = k * (1.0 + (a - 1.0) * k_a.astype(f32))
    y = _rwkv7_scan(heads(r), heads(decay), heads(k), heads(v), -kk, kk * heads(a))
    mu = jnp.mean(y, axis=-1, keepdims=True)
    var = jnp.mean(jnp.square(y - mu), axis=-1, keepdims=True)
    y = ((y - mu) * lax.rsqrt(var + RW_LN_EPS)).reshape(b, l, d) * ln_w.astype(f32) + ln_b.astype(f32)
    bonus = jnp.sum(heads(r) * heads(k) * r_k.astype(f32), axis=-1, keepdims=True) * heads(v)
    y = (y + bonus.reshape(b, l, d)) * g
    return y.astype(h.dtype) @ w_out


def setup_inputs(seed: int = 0) -> dict:
    key = jax.random.key(seed)
    counter = [0]

    def nxt():
        counter[0] += 1
        return jax.random.fold_in(key, counter[0])

    def nrm(shape, scale):
        return jax.random.normal(nxt(), shape, jnp.float32) * scale

    def gain(shape):
        return 1.0 + nrm(shape, 0.05)

    d = D_MODEL
    mla_in_w = MLA_Q_RANK + MLA_KV_RANK + MLA_ROPE_DIM
    return {
        'x': nrm((BATCH, SEQ, d), 1.0),
        'meta_tokens': nrm((N_META, d), 1.0),
        'norm_gains': gain((DEPTH, 3, 2, d)),
        'ffn_w_in': nrm((DEPTH, 2, d, 2 * D_FF), d ** -0.5),
        'ffn_w_out': nrm((DEPTH, 2, D_FF, d), D_FF ** -0.5),
        'sb_w_in': nrm((N_SB, d, 3 * SB_HEADS * SB_HEAD_DIM), d ** -0.5),
        'sb_w_out': nrm((N_SB, SB_HEADS * SB_HEAD_DIM, d), (SB_HEADS * SB_HEAD_DIM) ** -0.5),
        'mla_w_in': nrm((N_MLA, d, mla_in_w), d ** -0.5),
        'mla_q_norm': gain((N_MLA, MLA_Q_RANK)),
        'mla_kv_norm': gain((N_MLA, MLA_KV_RANK)),
        'mla_w_uq': nrm((N_MLA, MLA_Q_RANK, MLA_HEADS * (MLA_NOPE_DIM + MLA_ROPE_DIM)), MLA_Q_RANK ** -0.5),
        'mla_w_ukv': nrm((N_MLA, MLA_KV_RANK, MLA_HEADS * (MLA_NOPE_DIM + MLA_V_DIM)), MLA_KV_RANK ** -0.5),
        'mla_w_out': nrm((N_MLA, MLA_HEADS * MLA_V_DIM, d), (MLA_HEADS * MLA_V_DIM) ** -0.5),
        'rw_mix': jax.random.uniform(nxt(), (N_RWKV, 6, d), jnp.float32),
        'rw_w_in': nrm((N_RWKV, 3, d, d), d ** -0.5),
        'rw_w0': jax.random.uniform(nxt(), (N_RWKV, d), jnp.float32, -6.0, -1.0),
        'rw_w1': nrm((N_RWKV, d, RW_DECAY_LORA), d ** -0.5),
        'rw_w2': nrm((N_RWKV, RW_DECAY_LORA, d), 0.3 * RW_DECAY_LORA ** -0.5),
        'rw_a0': nrm((N_RWKV, d), 0.1),
        'rw_a1': nrm((N_RWKV, d, RW_A_LORA), d ** -0.5),
        'rw_a2': nrm((N_RWKV, RW_A_LORA, d), 0.3 * RW_A_LORA ** -0.5),
        'rw_g1': nrm((N_RWKV, d, RW_GATE_LORA), d ** -0.5),
        'rw_g2': nrm((N_RWKV, RW_GATE_LORA, d), RW_GATE_LORA ** -0.5),
        'rw_k_k': 0.85 + nrm((N_RWKV, d), 0.05),
        'rw_k_a': gain((N_RWKV, d)),
        'rw_r_k': nrm((N_RWKV, RW_HEADS, RW_HEAD_DIM), 0.1),
        'rw_ln_w': gain((N_RWKV, d)),
        'rw_ln_b': nrm((N_RWKV, d), 0.01),
        'rw_w_out': nrm((N_RWKV, d, d), d ** -0.5),
    }


def reference(x, meta_tokens, norm_gains, ffn_w_in, ffn_w_out, sb_w_in, sb_w_out,
              mla_w_in, mla_q_norm, mla_kv_norm, mla_w_uq, mla_w_ukv, mla_w_out,
              rw_mix, rw_w_in, rw_w0, rw_w1, rw_w2, rw_a0, rw_a1, rw_a2, rw_g1, rw_g2,
              rw_k_k, rw_k_a, rw_r_k, rw_ln_w, rw_ln_b, rw_w_out):
    b = x.shape[0]
    meta = jnp.broadcast_to(meta_tokens.astype(x.dtype)[None], (b, N_META, D_MODEL))
    h = jnp.concatenate([meta, x], axis=1)
    for i in range(DEPTH):
        g = norm_gains[i]
        h = h + 0.5 * _rmsnorm(_swiglu(_rmsnorm(h, g[0, 0]), ffn_w_in[i, 0], ffn_w_out[i, 0]), g[0, 1])
        u = _rmsnorm(h, g[1, 0])
        kind, slot = i % N_MIXERS, i // N_MIXERS
        if kind == 0:
            m = _stick_breaking_mixer(u, sb_w_in[slot], sb_w_out[slot])
        elif kind == 1:
            m = _mla_mixer(u, mla_w_in[slot], mla_q_norm[slot], mla_kv_norm[slot],
                           mla_w_uq[slot], mla_w_ukv[slot], mla_w_out[slot])
        else:
            m = _rwkv7_mixer(u, rw_mix[slot], rw_w_in[slot], rw_w0[slot], rw_w1[slot], rw_w2[slot],
                             rw_a0[slot], rw_a1[slot], rw_a2[slot], rw_g1[slot], rw_g2[slot],
                             rw_k_k[slot], rw_k_a[slot], rw_r_k[slot], rw_ln_w[slot], rw_ln_b[slot],
                             rw_w_out[slot])
        h = h + _rmsnorm(m, g[1, 1])
        h = h + 0.5 * _rmsnorm(_swiglu(_rmsnorm(h, g[2, 0]), ffn_w_in[i, 1], ffn_w_out[i, 1]), g[2, 1])
    return h[:, N_META:]
```

```python
import functools
import math

import jax
import jax.numpy as jnp
from jax import lax
from jax.experimental import pallas as pl
from jax.experimental.pallas import tpu as pltpu

F32 = jnp.float32
BF16 = jnp.bfloat16

D_MODEL = 1024
N_META = 16
RMS_EPS = 1e-6
SB_HEADS = 8
SB_HEAD_DIM = D_MODEL // SB_HEADS
SB_SCALE = 1.0 / math.sqrt(SB_HEAD_DIM)
MLA_HEADS = 8
MLA_NOPE_DIM = 128
MLA_ROPE_DIM = 64
MLA_V_DIM = 128
MLA_Q_RANK = 3 * D_MODEL // 8
MLA_KV_RANK = D_MODEL // 4
MLA_SCALE = 1.0 / math.sqrt(MLA_NOPE_DIM + MLA_ROPE_DIM)
ROPE_THETA = 10000.0
RW_HEAD_DIM = 64
RW_LN_EPS = 64e-5
D_FF = ((8 * D_MODEL // 3 + 127) // 128) * 128

LANES = 128
MXU_DIM = 256
PAD = 256
FIRST_VALID = PAD - N_META
ATT_BLOCK = MXU_DIM
FFN_CHUNK = MXU_DIM
RW_CHUNK = RW_HEAD_DIM
RW_GROUP = MXU_DIM
RW_CHUNKS_PER_STEP = 4
RW_PROJ_ROWS = 256
MASK_VALUE = -1e30
VMEM_LIMIT = 56 * 1024 * 1024


def _row_tile(lp):
    for t in (640, 512, 256, 128):
        if lp % t == 0:
            return t
    raise ValueError(f"unsupported padded length {lp}")


def _params(*sem):
    return pltpu.CompilerParams(dimension_semantics=sem, vmem_limit_bytes=VMEM_LIMIT)


def _resident(shape):
    nd = len(shape)
    return pl.BlockSpec(shape, lambda *_: (0,) * nd, pipeline_mode=pl.Buffered(1))


def _rms(x, g):
    return x * lax.rsqrt(jnp.mean(x * x, axis=-1, keepdims=True) + RMS_EPS) * g


def _dot(a, b):
    return jnp.dot(a, b, preferred_element_type=F32)


def _dot_nt(a, b):
    return lax.dot_general(a, b, (((1,), (1,)), ((), ())), preferred_element_type=F32)


def _dot_tn(a, b):
    return lax.dot_general(a, b, (((0,), (0,)), ((), ())), preferred_element_type=F32)


def _split(x):
    hi = x.astype(BF16)
    lo = (x - hi.astype(F32)).astype(BF16)
    return hi, lo


def _dot3(a, b, dot=_dot):
    ah, al = _split(a)
    bh, bl = _split(b)
    return dot(ah, bh) + (dot(al, bh) + dot(ah, bl))


def _dot2(a, ones):
    ah, al = _split(a)
    return _dot(ah, ones) + _dot(al, ones)


def _row_valid(tile_rows, lp):
    pos0 = (pl.program_id(0) * tile_rows) % lp
    pos = pos0 + lax.broadcasted_iota(jnp.int32, (tile_rows, 1), 0)
    return pos >= FIRST_VALID


def _ffn_body(x_ref, gpre_ref, gpost_ref, win_ref, wout_ref, o_ref):
    x = x_ref[...]
    xn = _rms(x, gpre_ref[...]).astype(BF16)
    acc = jnp.zeros(x.shape, F32)
    for c in range(D_FF // FFN_CHUNK):
        lo = c * FFN_CHUNK
        gate = _dot(xn, win_ref[:, lo:lo + FFN_CHUNK])
        up = _dot(xn, win_ref[:, D_FF + lo:D_FF + lo + FFN_CHUNK])
        act = (gate * jax.nn.sigmoid(gate) * up).astype(BF16)
        acc = acc + _dot(act, wout_ref[lo:lo + FFN_CHUNK, :])
    o_ref[...] = x + 0.5 * _rms(acc, gpost_ref[...])


def _ffn(h2, g_pre, g_post, w_in, w_out, tm):
    n, d = h2.shape
    row = pl.BlockSpec((tm, d), lambda i: (i, 0))
    return pl.pallas_call(
        _ffn_body,
        grid=(n // tm,),
        in_specs=[row, _resident((1, d)), _resident((1, d)), _resident(w_in.shape), _resident(w_out.shape)],
        out_specs=row,
        out_shape=jax.ShapeDtypeStruct((n, d), F32),
        compiler_params=_params("parallel"),
        name="ffn",
    )(h2, g_pre, g_post, w_in, w_out)


def _out_proj_body(o_ref, h_ref, g_ref, w_ref, out_ref, *, lp):
    tm = h_ref.shape[0]
    m = _dot(o_ref[...], w_ref[...])
    upd = jnp.where(_row_valid(tm, lp), _rms(m, g_ref[...]), 0.0)
    out_ref[...] = h_ref[...] + upd


def _out_proj(o2, h2, g_post, w_out, tm, lp):
    n, d = h2.shape
    return pl.pallas_call(
        functools.partial(_out_proj_body, lp=lp),
        grid=(n // tm,),
        in_specs=[pl.BlockSpec((tm, o2.shape[1]), lambda i: (i, 0)), pl.BlockSpec((tm, d), lambda i: (i, 0)),
                  _resident((1, d)), _resident(w_out.shape)],
        out_specs=pl.BlockSpec((tm, d), lambda i: (i, 0)),
        out_shape=jax.ShapeDtypeStruct((n, d), F32),
        compiler_params=_params("parallel"),
        name="out_proj",
    )(o2, h2, g_post, w_out)


def _sb_proj_body(x_ref, g_ref, w_ref, qkv_ref):
    u = _rms(x_ref[...], g_ref[...]).astype(BF16)
    qkv = _dot(u, w_ref[...])
    d = x_ref.shape[1]
    qkv_ref[:, :d] = (qkv[:, :d] * SB_SCALE).astype(BF16)
    qkv_ref[:, d:] = qkv[:, d:].astype(BF16)


def _sb_proj(h2, g_pre, w_in, tm):
    n, d = h2.shape
    return pl.pallas_call(
        _sb_proj_body,
        grid=(n // tm,),
        in_specs=[pl.BlockSpec((tm, d), lambda i: (i, 0)), _resident((1, d)), _resident(w_in.shape)],
        out_specs=pl.BlockSpec((tm, w_in.shape[1]), lambda i: (i, 0)),
        out_shape=jax.ShapeDtypeStruct((n, w_in.shape[1]), BF16),
        compiler_params=_params("parallel"),
        name="sb_proj",
    )(h2, g_pre, w_in)


def _sb_attn_body(q_ref, k_ref, v_ref, tri_ref, o_ref, carry_ref, acc_ref):
    tb = ATT_BLOCK
    qi = pl.program_id(2)
    q = q_ref[0]
    row = qi * tb + lax.broadcasted_iota(jnp.int32, (tb, tb), 0)
    carry_ref[...] = jnp.zeros_like(carry_ref)
    acc_ref[...] = jnp.zeros_like(acc_ref)

    def block(kb, masked):
        k0 = pl.multiple_of(kb * tb, tb)
        k = k_ref[0, pl.ds(k0, tb), :]
        v = v_ref[0, pl.ds(k0, tb), :]
        z = _dot_nt(q, k)
        sp = jnp.maximum(z, 0.0) + jnp.log(1.0 + jnp.exp(-jnp.abs(z)))
        if masked:
            col = k0 + lax.broadcasted_iota(jnp.int32, (tb, tb), 1)
            valid = (col < row) & (col >= FIRST_VALID)
            sp = jnp.where(valid, sp, 0.0)
        within = _dot(sp.astype(BF16), tri_ref[...])
        carry = carry_ref[...]
        p = jnp.exp(z - within - jnp.concatenate([carry] * (tb // LANES), axis=1))
        if masked:
            p = jnp.where(valid, p, 0.0)
        acc_ref[...] += _dot(p.astype(BF16), v)
        carry_ref[...] = carry + jnp.broadcast_to(within[:, 0:1], carry.shape)

    block(qi, True)

    def mid(j, _):
        block(qi - 1 - j, False)
        return 0

    lax.fori_loop(0, jnp.maximum(qi - 1, 0), mid, 0)

    @pl.when(qi >= 1)
    def _():
        block(0, True)

    o_ref[0] = acc_ref[...].astype(o_ref.dtype)


def _sb_attn(qkv, tri):
    b, lp, _ = qkv.shape
    tb = ATT_BLOCK
    hd = SB_HEAD_DIM
    return pl.pallas_call(
        _sb_attn_body,
        grid=(b, SB_HEADS, lp // tb),
        in_specs=[pl.BlockSpec((1, tb, hd), lambda bi, h, i: (bi, i, h)),
                  pl.BlockSpec((1, lp, hd), lambda bi, h, i: (bi, 0, SB_HEADS + h)),
                  pl.BlockSpec((1, lp, hd), lambda bi, h, i: (bi, 0, 2 * SB_HEADS + h)),
                  _resident(tri.shape)],
        out_specs=pl.BlockSpec((1, tb, hd), lambda bi, h, i: (bi, i, h)),
        out_shape=jax.ShapeDtypeStruct((b, lp, SB_HEADS * hd), BF16),
        scratch_shapes=[pltpu.VMEM((tb, LANES), F32), pltpu.VMEM((tb, hd), F32)],
        compiler_params=_params("parallel", "parallel", "arbitrary"),
        name="sb_attn",
    )(qkv, qkv, qkv, tri)


def _stick_breaking_mixer(h2, g_pre, g_post, w_in, w_out, b, lp, tm):
    qkv = _sb_proj(h2, g_pre, w_in.astype(BF16), tm)
    idx = jnp.arange(ATT_BLOCK)
    tri = (idx[:, None] >= idx[None, :]).astype(BF16)
    o = _sb_attn(qkv.reshape(b, lp, -1), tri)
    return _out_proj(o.reshape(b * lp, -1), h2, g_post, w_out.astype(BF16), tm, lp)


MLA_QK_DIM = MXU_DIM


def _rope_tile(t, cos, sin):
    return t * cos + pltpu.roll(t, MLA_ROPE_DIM, 1) * sin


def _mla_proj_body(x_ref, g_ref, win_ref, qn_ref, kvn_ref, wuq_ref, wukv_ref, cos_ref, sin_ref, q_ref, k_ref, v_ref):
    u = _rms(x_ref[...], g_ref[...]).astype(BF16)
    c = _dot(u, win_ref[...])
    cq = _rms(c[:, :MLA_Q_RANK], qn_ref[...]).astype(BF16)
    ckv = _rms(c[:, MLA_Q_RANK:MLA_Q_RANK + MLA_KV_RANK], kvn_ref[...]).astype(BF16)
    cos = cos_ref[...]
    sin = sin_ref[...]
    k_rope = _rope_tile(c[:, MLA_Q_RANK + MLA_KV_RANK:], cos, sin).astype(BF16)
    q = _dot(cq, wuq_ref[...])
    kv = _dot(ckv, wukv_ref[...])
    for h in range(MLA_HEADS):
        lo = h * MLA_QK_DIM
        q_ref[:, lo:lo + LANES] = q[:, lo:lo + LANES].astype(BF16)
        q_ref[:, lo + LANES:lo + 2 * LANES] = _rope_tile(q[:, lo + LANES:lo + 2 * LANES], cos, sin).astype(BF16)
        k_ref[:, lo:lo + LANES] = kv[:, h * LANES:(h + 1) * LANES].astype(BF16)
        k_ref[:, lo + LANES:lo + 2 * LANES] = k_rope
    v_ref[...] = kv[:, MLA_HEADS * MLA_NOPE_DIM:].astype(BF16)


def _mla_proj(h2, g_pre, w_in, q_norm, kv_norm, w_uq, w_ukv, cos, sin, tm, lp):
    n, d = h2.shape
    tiles_per_seq = lp // tm
    row = lambda w: pl.BlockSpec((tm, w), lambda i: (i, 0))
    table = pl.BlockSpec((tm, LANES), lambda i: (i % tiles_per_seq, 0))
    qk_w = MLA_HEADS * MLA_QK_DIM
    v_w = MLA_HEADS * MLA_V_DIM
    return pl.pallas_call(
        _mla_proj_body,
        grid=(n // tm,),
        in_specs=[row(d), _resident((1, d)), _resident(w_in.shape), _resident(q_norm.shape), _resident(kv_norm.shape),
                  _resident(w_uq.shape), _resident(w_ukv.shape), table, table],
        out_specs=[row(qk_w), row(qk_w), row(v_w)],
        out_shape=[jax.ShapeDtypeStruct((n, qk_w), BF16), jax.ShapeDtypeStruct((n, qk_w), BF16),
                   jax.ShapeDtypeStruct((n, v_w), BF16)],
        compiler_params=_params("parallel"),
        name="mla_proj",
    )(h2, g_pre, w_in, q_norm, kv_norm, w_uq, w_ukv, cos, sin)


def _mla_attn_body(q_ref, k_ref, v_ref, o_ref, m_ref, l_ref, acc_ref):
    tb = ATT_BLOCK
    qi = pl.program_id(2)
    q = q_ref[0]
    row = qi * tb + lax.broadcasted_iota(jnp.int32, (tb, tb), 0)
    m_ref[...] = jnp.full_like(m_ref, MASK_VALUE)
    l_ref[...] = jnp.zeros_like(l_ref)
    acc_ref[...] = jnp.zeros_like(acc_ref)
    reps = tb // LANES

    def block(kb, masked):
        k0 = pl.multiple_of(kb * tb, tb)
        k = k_ref[0, pl.ds(k0, tb), :]
        v = v_ref[0, pl.ds(k0, tb), :]
        z = _dot_nt(q, k) * MLA_SCALE
        if masked:
            col = k0 + lax.broadcasted_iota(jnp.int32, (tb, tb), 1)
            z = jnp.where((col <= row) & (col >= FIRST_VALID), z, MASK_VALUE)
        m_prev = m_ref[...]
        m_new = jnp.maximum(m_prev, jnp.max(z, axis=1, keepdims=True))
        p = jnp.exp(z - jnp.concatenate([m_new] * reps, axis=1))
        alpha = jnp.exp(m_prev - m_new)
        l_ref[...] = alpha * l_ref[...] + jnp.sum(p, axis=1, keepdims=True)
        acc_ref[...] = alpha * acc_ref[...] + _dot(p.astype(BF16), v)
        m_ref[...] = m_new

    block(qi, True)

    def mid(j, _):
        block(qi - 1 - j, False)
        return 0

    lax.fori_loop(0, jnp.maximum(qi - 1, 0), mid, 0)

    @pl.when(qi >= 1)
    def _():
        block(0, True)

    o_ref[0] = (acc_ref[...] / l_ref[...]).astype(o_ref.dtype)


def _mla_attn(q, k, v):
    b, lp, _ = q.shape
    tb = ATT_BLOCK
    return pl.pallas_call(
        _mla_attn_body,
        grid=(b, MLA_HEADS, lp // tb),
        in_specs=[pl.BlockSpec((1, tb, MLA_QK_DIM), lambda bi, h, i: (bi, i, h)),
                  pl.BlockSpec((1, lp, MLA_QK_DIM), lambda bi, h, i: (bi, 0, h)),
                  pl.BlockSpec((1, lp, MLA_V_DIM), lambda bi, h, i: (bi, 0, h))],
        out_specs=pl.BlockSpec((1, tb, MLA_V_DIM), lambda bi, h, i: (bi, i, h)),
        out_shape=jax.ShapeDtypeStruct((b, lp, MLA_HEADS * MLA_V_DIM), BF16),
        scratch_shapes=[pltpu.VMEM((tb, LANES), F32), pltpu.VMEM((tb, LANES), F32), pltpu.VMEM((tb, MLA_V_DIM), F32)],
        compiler_params=_params("parallel", "parallel", "arbitrary"),
        name="mla_attn",
    )(q, k, v)


def _rotate_half_cols(w):
    half = MLA_ROPE_DIM // 2
    return jnp.concatenate([-w[..., half:], w[..., :half]], axis=-1)


def _mla_mixer(h2, g_pre, g_post, w_in, q_norm, kv_norm, w_uq, w_ukv, w_out, b, lp, tm):
    rank = MLA_Q_RANK + MLA_KV_RANK
    w_kr = w_in[:, rank:]
    w_in_x = jnp.concatenate([w_in[:, :rank], w_kr, _rotate_half_cols(w_kr)], axis=1).astype(BF16)
    wq = w_uq.reshape(MLA_Q_RANK, MLA_HEADS, MLA_NOPE_DIM + MLA_ROPE_DIM)
    wq_rope = wq[..., MLA_NOPE_DIM:]
    w_uq_x = jnp.concatenate([wq[..., :MLA_NOPE_DIM], wq_rope, _rotate_half_cols(wq_rope)], axis=-1)
    w_uq_x = w_uq_x.reshape(MLA_Q_RANK, MLA_HEADS * MLA_QK_DIM).astype(BF16)
    wkv = w_ukv.reshape(MLA_KV_RANK, MLA_HEADS, MLA_NOPE_DIM + MLA_V_DIM)
    w_ukv_x = jnp.concatenate([wkv[..., :MLA_NOPE_DIM].reshape(MLA_KV_RANK, -1),
                               wkv[..., MLA_NOPE_DIM:].reshape(MLA_KV_RANK, -1)], axis=1).astype(BF16)
    half = MLA_ROPE_DIM // 2
    inv_freq = ROPE_THETA ** (-jnp.arange(half, dtype=F32) / half)
    pos = jnp.arange(lp) - FIRST_VALID
    ang = pos.astype(F32)[:, None] * inv_freq[None, :]
    zeros = jnp.zeros((lp, LANES - MLA_ROPE_DIM), F32)
    cos = jnp.concatenate([jnp.cos(ang), jnp.cos(ang), zeros], axis=1)
    sin = jnp.concatenate([jnp.sin(ang), jnp.sin(ang), zeros], axis=1)
    q, k, v = _mla_proj(h2, g_pre, w_in_x, q_norm[None], kv_norm[None], w_uq_x, w_ukv_x, cos, sin, tm, lp)
    o = _mla_attn(q.reshape(b, lp, -1), k.reshape(b, lp, -1), v.reshape(b, lp, -1))
    return _out_proj(o.reshape(b * lp, -1), h2, g_post, w_out.astype(BF16), tm, lp)


def _head_sum(x, bd_ref):
    return _dot2(x, bd_ref[...])


def _rw_proj_body(x_ref, xp_ref, g_ref, mix_ref, wrkv_ref, w0_ref, w1_ref, w2_ref, a0_ref, a1_ref, a2_ref,
                  g1_ref, g2_ref, kk_ref, ka_ref, rk_ref, bd_ref,
                  r_ref, lw_ref, k_ref, v_ref, na_ref, b_ref, gate_ref, bonus_ref, *, lp):
    tm = x_ref.shape[0]
    g = g_ref[...]
    u = _rms(x_ref[...], g)
    first = (pl.program_id(0) * tm) % lp == 0
    u_prev = jnp.where(first, 0.0, _rms(xp_ref[...], g)[7:8, :])
    shifted = jnp.where(lax.broadcasted_iota(jnp.int32, (tm, 1), 0) == 0, u_prev, pltpu.roll(u, 1, 0))
    xx = shifted - u
    mix = mix_ref[...]
    xs = lambda n: (u + xx * mix[n:n + 1, :]).astype(BF16)
    r = _dot(xs(0), wrkv_ref[0])
    k = _dot(xs(1), wrkv_ref[1])
    v = _dot(xs(2), wrkv_ref[2])
    wl = w0_ref[...] + _dot(jnp.tanh(_dot(xs(3), w1_ref[...])).astype(BF16), w2_ref[...])
    w_log = -(jnp.maximum(-wl, 0.0) + jnp.log(1.0 + jnp.exp(-jnp.abs(wl)))) - 0.5
    lw_ref[...] = -jnp.exp(w_log)
    a = jax.nn.sigmoid(a0_ref[...] + _dot(_dot(xs(4), a1_ref[...]).astype(BF16), a2_ref[...]))
    gate_ref[...] = _dot(jax.nn.sigmoid(_dot(xs(5), g1_ref[...])).astype(BF16), g2_ref[...])
    kk = k * kk_ref[...]
    kk = kk / jnp.maximum(jnp.sqrt(_head_sum(kk * kk, bd_ref)), 1e-12)
    k2 = k * (1.0 + (a - 1.0) * ka_ref[...])
    r_ref[...] = r
    k_ref[...] = k2
    v_ref[...] = v
    na_ref[...] = -kk
    b_ref[...] = kk * a
    bonus_ref[...] = _head_sum(r * k2 * rk_ref[...], bd_ref) * v


def _rw_proj(h2, g_pre, mix, w_rkv, w0, w1, w2, a0, a1, a2, g1, g2, k_k, k_a, r_k, bd, tm, lp):
    n, d = h2.shape
    row = pl.BlockSpec((tm, d), lambda i: (i, 0))
    prev = pl.BlockSpec((8, d), lambda i: (jnp.maximum(i * (tm // 8) - 1, 0), 0))
    vec = _resident((1, d))
    out = jax.ShapeDtypeStruct((n, d), F32)
    return pl.pallas_call(
        functools.partial(_rw_proj_body, lp=lp),
        grid=(n // tm,),
        in_specs=[row, prev, vec, _resident(mix.shape), _resident(w_rkv.shape), vec, _resident(w1.shape),
                  _resident(w2.shape), vec, _resident(a1.shape), _resident(a2.shape), _resident(g1.shape),
                  _resident(g2.shape), vec, vec, vec, _resident(bd.shape)],
        out_specs=[row] * 8,
        out_shape=[out] * 8,
        compiler_params=_params("parallel"),
        name="rw_proj",
    )(h2, h2, g_pre, mix, w_rkv, w0, w1, w2, a0, a1, a2, g1, g2, k_k, k_a, r_k, bd)


def _rw_scan_body(r_ref, lw_ref, k_ref, v_ref, a_ref, b_ref, y_ref, s_ref):
    c = RW_CHUNK
    gw = RW_GROUP
    heads = gw // c

    @pl.when(pl.program_id(2) == 0)
    def _():
        s_ref[...] = jnp.zeros_like(s_ref)

    ri = lax.broadcasted_iota(jnp.int32, (gw, gw), 0)
    ci = lax.broadcasted_iota(jnp.int32, (gw, gw), 1)
    shift = int(math.log2(c))
    block_mask = (ri >> shift) == (ci >> shift)
    t_idx = lax.broadcasted_iota(jnp.int32, (c, gw), 0)
    s_idx = lax.broadcasted_iota(jnp.int32, (c, gw), 1) & (c - 1)
    strict = s_idx < t_idx
    incl = s_idx <= t_idx
    eye = (s_idx == t_idx).astype(F32)
    tri = (lax.broadcasted_iota(jnp.int32, (c, c), 1) <= lax.broadcasted_iota(jnp.int32, (c, c), 0)).astype(BF16)

    def bd(x):
        return jnp.where(block_mask, jnp.concatenate([x] * heads, axis=0), 0.0)

    for j in range(RW_CHUNKS_PER_STEP):
        rows = slice(j * c, (j + 1) * c)
        r, lw, k, v, a, b = (ref[0, rows, :] for ref in (r_ref, lw_ref, k_ref, v_ref, a_ref, b_ref))
        cum = _cumsum_rows(lw, tri)
        g_in = jnp.exp(cum)
        g_ex = jnp.exp(cum - lw)
        g_inv = jnp.exp(-cum)
        total = cum[c - 1:c, :]
        g_out = jnp.exp(total - cum)
        lhs = jnp.concatenate([a * g_ex, r * g_in], axis=0)
        a_b = _dot3(lhs, bd(b * g_inv), _dot_nt)
        a_k = _dot3(lhs, bd(k * g_inv), _dot_nt)
        n_mat = jnp.where(strict, a_b[:c], 0.0)
        t_mat = eye + n_mat
        pw = n_mat
        for _ in range(int(math.log2(c)) - 1):
            pw = _dot3(pw, bd(pw))
            t_mat = t_mat + _dot3(t_mat, bd(pw))
        s_t = s_ref[...]
        from_state = _dot3(lhs, s_t, _dot_nt)
        bd_v = bd(v)
        x = from_state[:c] + _dot3(jnp.where(strict, a_k[:c], 0.0), bd_v)
        u = _dot3(t_mat, bd(x))
        y = from_state[c:] + _dot3(jnp.where(incl, a_b[c:], 0.0), bd(u)) + _dot3(jnp.where(incl, a_k[c:], 0.0), bd_v)
        y_ref[0, rows, :] = y
        upd = _dot3(jnp.concatenate([u, v], axis=0), jnp.concatenate([b * g_out, k * g_out], axis=0), _dot_tn)
        s_ref[...] = s_t * jnp.exp(total) + jnp.where(block_mask, upd, 0.0)


def _cumsum_rows(x, tri):
    hi, lo = _split(x)
    return _dot(tri, hi) + _dot(tri, lo)


def _rw_scan(r, lw, k, v, a, b):
    bsz, lp, d = r.shape
    rows = RW_CHUNK * RW_CHUNKS_PER_STEP
    spec = pl.BlockSpec((1, rows, RW_GROUP), lambda bi, g, i: (bi, i, g))
    return pl.pallas_call(
        _rw_scan_body,
        grid=(bsz, d // RW_GROUP, lp // rows),
        in_specs=[spec] * 6,
        out_specs=spec,
        out_shape=jax.ShapeDtypeStruct((bsz, lp, d), F32),
        scratch_shapes=[pltpu.VMEM((RW_GROUP, RW_GROUP), F32)],
        compiler_params=_params("parallel", "parallel", "arbitrary"),
        name="rw_scan",
    )(r, lw, k, v, a, b)


def _rw_out_body(y_ref, bonus_ref, gate_ref, h_ref, lnw_ref, lnb_ref, g_ref, w_ref, bd_ref, out_ref, *, lp):
    tm = h_ref.shape[0]
    y = y_ref[...]
    mu = _head_sum(y, bd_ref) * (1.0 / RW_HEAD_DIM)
    dlt = y - mu
    var = _head_sum(dlt * dlt, bd_ref) * (1.0 / RW_HEAD_DIM)
    yn = dlt * lax.rsqrt(var + RW_LN_EPS) * lnw_ref[...] + lnb_ref[...]
    z = ((yn + bonus_ref[...]) * gate_ref[...]).astype(BF16)
    m = _dot(z, w_ref[...])
    upd = jnp.where(_row_valid(tm, lp), _rms(m, g_ref[...]), 0.0)
    out_ref[...] = h_ref[...] + upd


def _rw_out(y2, bonus, gate, h2, ln_w, ln_b, g_post, w_out, bd, tm, lp):
    n, d = h2.shape
    row = pl.BlockSpec((tm, d), lambda i: (i, 0))
    vec = _resident((1, d))
    return pl.pallas_call(
        functools.partial(_rw_out_body, lp=lp),
        grid=(n // tm,),
        in_specs=[row, row, row, row, vec, vec, vec, _resident(w_out.shape), _resident(bd.shape)],
        out_specs=row,
        out_shape=jax.ShapeDtypeStruct((n, d), F32),
        compiler_params=_params("parallel"),
        name="rw_out",
    )(y2, bonus, gate, h2, ln_w, ln_b, g_post, w_out, bd)


def _pad_cols(w, width):
    return jnp.pad(w, ((0, 0), (0, width - w.shape[1])))


def _pad_rows(w, height):
    return jnp.pad(w, ((0, height - w.shape[0]), (0, 0)))


def _rwkv7_mixer(h2, g_pre, g_post, mix, w_in, w0, w1, w2, a0, a1, a2, g1, g2, k_k, k_a, r_k, ln_w, ln_b, w_out,
                 b, lp, tm):
    d = h2.shape[1]
    idx = jnp.arange(d) // RW_HEAD_DIM
    bd = (idx[:, None] == idx[None, :]).astype(BF16)
    lora = lambda w_a, w_b: (_pad_cols(w_a, -(-w_a.shape[1] // LANES) * LANES).astype(BF16),
                             _pad_rows(w_b, -(-w_b.shape[0] // LANES) * LANES).astype(BF16))
    w1p, w2p = lora(w1, w2)
    a1p, a2p = lora(a1, a2)
    g1p, g2p = lora(g1, g2)
    outs = _rw_proj(h2, g_pre, mix, w_in.astype(BF16), w0[None], w1p, w2p, a0[None], a1p, a2p, g1p, g2p,
                    k_k[None], k_a[None], r_k.reshape(1, d), bd, RW_PROJ_ROWS, lp)
    r, lw, k, v, na, bb, gate, bonus = outs
    y = _rw_scan(*(t.reshape(b, lp, d) for t in (r, lw, k, v, na, bb)))
    return _rw_out(y.reshape(b * lp, d), bonus, gate, h2, ln_w[None], ln_b[None], g_post, w_out.astype(BF16), bd, tm, lp)


def kernel(x, meta_tokens, norm_gains, ffn_w_in, ffn_w_out, sb_w_in, sb_w_out, mla_w_in, mla_q_norm, mla_kv_norm, mla_w_uq, mla_w_ukv, mla_w_out, rw_mix, rw_w_in, rw_w0, rw_w1, rw_w2, rw_a0, rw_a1, rw_a2, rw_g1, rw_g2, rw_k_k, rw_k_a, rw_r_k, rw_ln_w, rw_ln_b, rw_w_out):
    b, seq, d = x.shape
    assert d == D_MODEL and seq % ATT_BLOCK == 0
    lp = PAD + seq
    tm = _row_tile(lp)
    meta = jnp.broadcast_to(meta_tokens.astype(x.dtype)[None], (b, N_META, d))
    h = jnp.concatenate([jnp.zeros((b, FIRST_VALID, d), x.dtype), meta, x], axis=1)
    h2 = h.reshape(b * lp, d)
    depth = norm_gains.shape[0]
    for i in range(depth):
        g = norm_gains[i][:, :, None, :]
        h2 = _ffn(h2, g[0, 0], g[0, 1], ffn_w_in[i, 0].astype(BF16), ffn_w_out[i, 0].astype(BF16), tm)
        kind, slot = i % 3, i // 3
        if kind == 0:
            h2 = _stick_breaking_mixer(h2, g[1, 0], g[1, 1], sb_w_in[slot], sb_w_out[slot], b, lp, tm)
        elif kind == 1:
            h2 = _mla_mixer(h2, g[1, 0], g[1, 1], mla_w_in[slot], mla_q_norm[slot], mla_kv_norm[slot],
                            mla_w_uq[slot], mla_w_ukv[slot], mla_w_out[slot], b, lp, tm)
        else:
            h2 = _rwkv7_mixer(h2, g[1, 0], g[1, 1], rw_mix[slot], rw_w_in[slot], rw_w0[slot], rw_w1[slot],
                              rw_w2[slot], rw_a0[slot], rw_a1[slot], rw_a2[slot], rw_g1[slot], rw_g2[slot],
                              rw_k_k[slot], rw_k_a[slot], rw_r_k[slot], rw_ln_w[slot], rw_ln_b[slot],
                              rw_w_out[slot], b, lp, tm)
        h2 = _ffn(h2, g[2, 0], g[2, 1], ffn_w_in[i, 1].astype(BF16), ffn_w_out[i, 1].astype(BF16), tm)
    return h2.reshape(b, lp, d)[:, PAD:]
```

```python
import functools
import math

import jax
import jax.numpy as jnp
from jax import lax
from jax.experimental import pallas as pl
from jax.experimental.pallas import tpu as pltpu

F32 = jnp.float32
BF16 = jnp.bfloat16

D_MODEL = 1024
N_META = 16
RMS_EPS = 1e-6
SB_HEADS = 8
SB_HEAD_DIM = D_MODEL // SB_HEADS
SB_SCALE = 1.0 / math.sqrt(SB_HEAD_DIM)
MLA_HEADS = 8
MLA_NOPE_DIM = 128
MLA_ROPE_DIM = 64
MLA_V_DIM = 128
MLA_Q_RANK = 3 * D_MODEL // 8
MLA_KV_RANK = D_MODEL // 4
MLA_SCALE = 1.0 / math.sqrt(MLA_NOPE_DIM + MLA_ROPE_DIM)
ROPE_THETA = 10000.0
RW_HEAD_DIM = 64
RW_LN_EPS = 64e-5
D_FF = ((8 * D_MODEL // 3 + 127) // 128) * 128

LANES = 128
MXU_DIM = 256
PAD = 256
FIRST_VALID = PAD - N_META
ATT_BLOCK = MXU_DIM
FFN_CHUNK = MXU_DIM
RW_CHUNK = RW_HEAD_DIM
RW_GROUP = MXU_DIM
RW_CHUNKS_PER_STEP = (10, 5, 4, 2, 1)
RW_PROJ_ROWS = 256
MASK_VALUE = -1e30
SB_UNDERFLOW_LOG = 110.0
VMEM_LIMIT = 56 * 1024 * 1024


def _row_tile(lp):
    for t in (640, 512, 256, 128):
        if lp % t == 0:
            return t
    raise ValueError(f"unsupported padded length {lp}")


def _params(*sem, flags=None):
    return pltpu.CompilerParams(dimension_semantics=sem, vmem_limit_bytes=VMEM_LIMIT, flags=flags)


def _resident(shape):
    nd = len(shape)
    return pl.BlockSpec(shape, lambda *_: (0,) * nd, pipeline_mode=pl.Buffered(1))


def _rms(x, g):
    return x * lax.rsqrt(jnp.mean(x * x, axis=-1, keepdims=True) + RMS_EPS) * g


def _dot(a, b):
    return jnp.dot(a, b, preferred_element_type=F32)


def _dot_nt(a, b):
    return lax.dot_general(a, b, (((1,), (1,)), ((), ())), preferred_element_type=F32)


def _dot_tn(a, b):
    return lax.dot_general(a, b, (((0,), (0,)), ((), ())), preferred_element_type=F32)


def _split(x):
    hi = x.astype(BF16)
    lo = (x - hi.astype(F32)).astype(BF16)
    return hi, lo


def _row_valid(tile_rows, lp):
    pos0 = (pl.program_id(0) * tile_rows) % lp
    pos = pos0 + lax.broadcasted_iota(jnp.int32, (tile_rows, 1), 0)
    return pos >= FIRST_VALID


def _ffn_body(x_ref, gpre_ref, gpost_ref, win_ref, wout_ref, o_ref):
    x = x_ref[...]
    xn = _rms(x, gpre_ref[...]).astype(BF16)
    acc = jnp.zeros(x.shape, F32)
    for c in range(D_FF // FFN_CHUNK):
        lo = c * FFN_CHUNK
        gate = _dot(xn, win_ref[:, lo:lo + FFN_CHUNK])
        up = _dot(xn, win_ref[:, D_FF + lo:D_FF + lo + FFN_CHUNK])
        act = (gate * jax.nn.sigmoid(gate) * up).astype(BF16)
        acc = acc + _dot(act, wout_ref[lo:lo + FFN_CHUNK, :])
    o_ref[...] = x + 0.5 * _rms(acc, gpost_ref[...])


def _ffn(h2, g_pre, g_post, w_in, w_out, tm):
    n, d = h2.shape
    row = pl.BlockSpec((tm, d), lambda i: (i, 0))
    return pl.pallas_call(
        _ffn_body,
        grid=(n // tm,),
        in_specs=[row, _resident((1, d)), _resident((1, d)), _resident(w_in.shape), _resident(w_out.shape)],
        out_specs=row,
        out_shape=jax.ShapeDtypeStruct((n, d), F32),
        compiler_params=_params("parallel"),
        name="ffn",
    )(h2, g_pre, g_post, w_in, w_out)


def _out_proj_body(o_ref, h_ref, g_ref, w_ref, out_ref, *, lp):
    tm = h_ref.shape[0]
    m = _dot(o_ref[...], w_ref[...])
    upd = jnp.where(_row_valid(tm, lp), _rms(m, g_ref[...]), 0.0)
    out_ref[...] = h_ref[...] + upd


def _out_proj(o2, h2, g_post, w_out, tm, lp):
    n, d = h2.shape
    return pl.pallas_call(
        functools.partial(_out_proj_body, lp=lp),
        grid=(n // tm,),
        in_specs=[pl.BlockSpec((tm, o2.shape[1]), lambda i: (i, 0)), pl.BlockSpec((tm, d), lambda i: (i, 0)),
                  _resident((1, d)), _resident(w_out.shape)],
        out_specs=pl.BlockSpec((tm, d), lambda i: (i, 0)),
        out_shape=jax.ShapeDtypeStruct((n, d), F32),
        compiler_params=_params("parallel"),
        name="out_proj",
    )(o2, h2, g_post, w_out)


def _sb_proj_body(x_ref, g_ref, w_ref, qkv_ref):
    u = _rms(x_ref[...], g_ref[...]).astype(BF16)
    qkv = _dot(u, w_ref[...])
    d = x_ref.shape[1]
    qkv_ref[:, :d] = (qkv[:, :d] * SB_SCALE).astype(BF16)
    qkv_ref[:, d:] = qkv[:, d:].astype(BF16)


def _sb_proj(h2, g_pre, w_in, tm):
    n, d = h2.shape
    return pl.pallas_call(
        _sb_proj_body,
        grid=(n // tm,),
        in_specs=[pl.BlockSpec((tm, d), lambda i: (i, 0)), _resident((1, d)), _resident(w_in.shape)],
        out_specs=pl.BlockSpec((tm, w_in.shape[1]), lambda i: (i, 0)),
        out_shape=jax.ShapeDtypeStruct((n, w_in.shape[1]), BF16),
        compiler_params=_params("parallel"),
        name="sb_proj",
    )(h2, g_pre, w_in)


def _walk_key_steps(top, front, back, state):
    def trip(masked):
        def body(_, carried):
            step, fr, st = carried
            st = back(step, masked, fr, st)
            return step - 1, front(step - 1), st
        return body

    carried = (top, front(top), state)
    carried = lax.fori_loop(0, jnp.minimum(top, 1), trip(True), carried)
    carried = lax.fori_loop(0, jnp.maximum(top - 1, 0), trip(False), carried)
    _, fr, st = carried
    return lax.cond(top == 0, lambda f, s: back(0, True, f, s), lambda f, s: back(0, False, f, s), fr, st)


def _reduce_rows(x, op, group=64):
    parts = [x[i:i + group] for i in range(0, x.shape[0], group)]
    while len(parts) > 1:
        parts = [op(parts[i], parts[i + 1]) for i in range(0, len(parts) - 1, 2)] + parts[len(parts) & ~1:]
    part = parts[0]
    return jnp.max(part, axis=0, keepdims=True) if op is jnp.maximum else jnp.sum(part, axis=0, keepdims=True)


def _lane_tile(x, reps):
    return x if reps == 1 else jnp.concatenate([x] * reps, axis=1)


def _sb_attn_body(q_ref, k_ref, v_ref, tri_ref, o_ref):
    tb = ATT_BLOCK
    qi = pl.program_id(2)
    q = q_ref[0]
    tri = tri_ref[...]
    sign = jnp.int32(-2 ** 31)

    def step(kb, masked, carry, acc):
        k0 = pl.multiple_of(kb * tb, tb)
        k = k_ref[0, pl.ds(k0, tb), :]
        v = v_ref[0, pl.ds(k0, tb), :]
        z = _dot_nt(q, k)
        neg_abs = lax.bitcast_convert_type(lax.bitcast_convert_type(z, jnp.int32) | sign, F32)
        sp = jnp.maximum(z, 0.0) + jnp.log(1.0 + jnp.exp(neg_abs))
        if masked:
            row = qi * tb + lax.broadcasted_iota(jnp.int32, (tb, tb), 0)
            col = k0 + lax.broadcasted_iota(jnp.int32, (tb, tb), 1)
            valid = (col < row) & (col >= FIRST_VALID)
            sp = jnp.where(valid, sp, 0.0)
        within = _dot(sp.astype(BF16), tri)
        p = jnp.exp(z - within - _lane_tile(carry, tb // LANES))
        if masked:
            p = jnp.where(valid, p, 0.0)
        return carry + jnp.broadcast_to(within[:, 0:1], carry.shape), acc + _dot(p.astype(BF16), v)

    def live(carry):
        return jnp.min(carry) < SB_UNDERFLOW_LOG

    last = FIRST_VALID // tb
    carry, acc = step(qi, True, jnp.zeros((tb, LANES), F32), jnp.zeros((tb, SB_HEAD_DIM), F32))
    kb, carry, acc = lax.while_loop(lambda c: (c[0] > last) & live(c[1]),
                                    lambda c: (c[0] - 1,) + step(c[0], False, c[1], c[2]),
                                    (qi - 1, carry, acc))
    carry, acc = lax.cond((kb == last) & live(carry), lambda c, a: step(last, True, c, a), lambda c, a: (c, a),
                          carry, acc)
    o_ref[0] = acc.astype(o_ref.dtype)


def _sb_attn(qkv, tri):
    b, lp, _ = qkv.shape
    tb = ATT_BLOCK
    hd = SB_HEAD_DIM
    return pl.pallas_call(
        _sb_attn_body,
        grid=(b, SB_HEADS, lp // tb),
        in_specs=[pl.BlockSpec((1, tb, hd), lambda bi, h, i: (bi, i, h)),
                  pl.BlockSpec((1, lp, hd), lambda bi, h, i: (bi, 0, SB_HEADS + h)),
                  pl.BlockSpec((1, lp, hd), lambda bi, h, i: (bi, 0, 2 * SB_HEADS + h)),
                  _resident(tri.shape)],
        out_specs=pl.BlockSpec((1, tb, hd), lambda bi, h, i: (bi, i, h)),
        out_shape=jax.ShapeDtypeStruct((b, lp, SB_HEADS * hd), BF16),
        compiler_params=_params("parallel", "parallel", "arbitrary"),
        name="sb_attn",
    )(qkv, qkv, qkv, tri)


def _stick_breaking_mixer(h2, g_pre, g_post, w_in, w_out, b, lp, tm):
    qkv = _sb_proj(h2, g_pre, w_in.astype(BF16), tm)
    idx = jnp.arange(ATT_BLOCK)
    tri = (idx[:, None] >= idx[None, :]).astype(BF16)
    o = _sb_attn(qkv.reshape(b, lp, -1), tri)
    return _out_proj(o.reshape(b * lp, -1), h2, g_post, w_out.astype(BF16), tm, lp)


MLA_QK_DIM = MXU_DIM
MLA_Q_SCALE = MLA_SCALE * math.log2(math.e)
MLA_KEY_BLOCKS = 4
MLA_Q_BLOCK = 1 * ATT_BLOCK


def _rope_tile(t, cos, sin):
    return t * cos + pltpu.roll(t, MLA_ROPE_DIM, 1) * sin


def _mla_proj_body(x_ref, g_ref, win_ref, qn_ref, kvn_ref, wuq_ref, wukv_ref, cos_ref, sin_ref, q_ref, k_ref, v_ref,
                   *, lp):
    u = _rms(x_ref[...], g_ref[...]).astype(BF16)
    c = _dot(u, win_ref[...])
    cq = _rms(c[:, :MLA_Q_RANK], qn_ref[...]).astype(BF16)
    ckv = _rms(c[:, MLA_Q_RANK:MLA_Q_RANK + MLA_KV_RANK], kvn_ref[...]).astype(BF16)
    cos = cos_ref[...]
    sin = sin_ref[...]
    tm = x_ref.shape[0]
    bias_lane = lax.broadcasted_iota(jnp.int32, (tm, LANES), 1) == MLA_ROPE_DIM
    q_bias = jnp.where(bias_lane, 1.0, 0.0)
    k_bias = jnp.where(bias_lane & jnp.logical_not(_row_valid(tm, lp)), MASK_VALUE, 0.0)
    k_rope = (_rope_tile(c[:, MLA_Q_RANK + MLA_KV_RANK:], cos, sin) + k_bias).astype(BF16)
    q = _dot(cq, wuq_ref[...])
    kv = _dot(ckv, wukv_ref[...])
    for h in range(MLA_HEADS):
        lo = h * MLA_QK_DIM
        q_ref[:, lo:lo + LANES] = (q[:, lo:lo + LANES] * MLA_Q_SCALE).astype(BF16)
        q_rope = _rope_tile(q[:, lo + LANES:lo + 2 * LANES], cos, sin)
        q_ref[:, lo + LANES:lo + 2 * LANES] = (q_rope * MLA_Q_SCALE + q_bias).astype(BF16)
        k_ref[:, lo:lo + LANES] = kv[:, h * LANES:(h + 1) * LANES].astype(BF16)
        k_ref[:, lo + LANES:lo + 2 * LANES] = k_rope
    v_ref[...] = kv[:, MLA_HEADS * MLA_NOPE_DIM:].astype(BF16)


def _mla_proj(h2, g_pre, w_in, q_norm, kv_norm, w_uq, w_ukv, cos, sin, tm, lp):
    n, d = h2.shape
    tiles_per_seq = lp // tm
    row = lambda w: pl.BlockSpec((tm, w), lambda i: (i, 0))
    table = pl.BlockSpec((tm, LANES), lambda i: (i % tiles_per_seq, 0))
    qk_w = MLA_HEADS * MLA_QK_DIM
    v_w = MLA_HEADS * MLA_V_DIM
    return pl.pallas_call(
        functools.partial(_mla_proj_body, lp=lp),
        grid=(n // tm,),
        in_specs=[row(d), _resident((1, d)), _resident(w_in.shape), _resident(q_norm.shape), _resident(kv_norm.shape),
                  _resident(w_uq.shape), _resident(w_ukv.shape), table, table],
        out_specs=[row(qk_w), row(qk_w), row(v_w)],
        out_shape=[jax.ShapeDtypeStruct((n, qk_w), BF16), jax.ShapeDtypeStruct((n, qk_w), BF16),
                   jax.ShapeDtypeStruct((n, v_w), BF16)],
        compiler_params=_params("parallel"),
        name="mla_proj",
    )(h2, g_pre, w_in, q_norm, kv_norm, w_uq, w_ukv, cos, sin)


def _mla_attn_body(q_ref, k_ref, vt_ref, o_ref):
    tb = ATT_BLOCK
    per_step = MLA_KEY_BLOCKS
    tk = per_step * tb
    tq = q_ref.shape[1]
    lp = k_ref.shape[1]
    qi = pl.program_id(2)
    q = q_ref[0]

    def first_block(step):
        return jnp.minimum(step * per_step, lp // tb - per_step)

    def front(step):
        k0 = pl.multiple_of(first_block(step) * tb, tb)
        return _dot_nt(k_ref[0, pl.ds(k0, tk), :], q)

    def back(step, masked, z, state):
        m_prev, l_prev, acc = state
        kb0 = first_block(step)
        if masked:
            key = kb0 * tb + lax.broadcasted_iota(jnp.int32, (tk, tq), 0)
            qry = qi * tq + lax.broadcasted_iota(jnp.int32, (tk, tq), 1)
            z = jnp.where((key <= qry) & (key >= jnp.maximum(step * tk, FIRST_VALID)), z, MASK_VALUE)
        m_new = jnp.maximum(m_prev, _reduce_rows(z, jnp.maximum))
        p = jnp.exp2(z - m_new)
        alpha = jnp.exp2(m_prev - m_new)
        l_new = alpha * l_prev + _reduce_rows(p, jnp.add)
        vt = jnp.concatenate([vt_ref[0, 0, kb0 + i] for i in range(per_step)], axis=1)
        return m_new, l_new, alpha * acc + _dot(vt, p.astype(BF16))

    init = (jnp.full((1, tq), MASK_VALUE, F32), jnp.zeros((1, tq), F32), jnp.zeros((MLA_V_DIM, tq), F32))
    top = lax.shift_right_logical(qi * tq + (tq - 1), int(math.log2(tk)))
    _, l, acc = _walk_key_steps(top, front, back, init)
    o_ref[0] = (acc / l).T.astype(o_ref.dtype)


def _mla_attn(q, k, vt):
    b, lp, _ = q.shape
    tb = ATT_BLOCK
    tq = MLA_Q_BLOCK
    assert lp % tq == 0 and lp >= MLA_KEY_BLOCKS * tb and FIRST_VALID < MLA_KEY_BLOCKS * tb
    return pl.pallas_call(
        _mla_attn_body,
        grid=(b, MLA_HEADS, lp // tq),
        in_specs=[pl.BlockSpec((1, tq, MLA_QK_DIM), lambda bi, h, i: (bi, i, h)),
                  pl.BlockSpec((1, lp, MLA_QK_DIM), lambda bi, h, i: (bi, 0, h)),
                  pl.BlockSpec((1, 1, lp // tb, MLA_V_DIM, tb), lambda bi, h, i: (bi, h, 0, 0, 0))],
        out_specs=pl.BlockSpec((1, tq, MLA_V_DIM), lambda bi, h, i: (bi, i, h)),
        out_shape=jax.ShapeDtypeStruct((b, lp, MLA_HEADS * MLA_V_DIM), BF16),
        compiler_params=_params("parallel", "parallel", "arbitrary"),
        name="mla_attn",
    )(q, k, vt)


def _rotate_half_cols(w):
    half = MLA_ROPE_DIM // 2
    return jnp.concatenate([-w[..., half:], w[..., :half]], axis=-1)


def _mla_mixer(h2, g_pre, g_post, w_in, q_norm, kv_norm, w_uq, w_ukv, w_out, b, lp, tm):
    rank = MLA_Q_RANK + MLA_KV_RANK
    w_kr = w_in[:, rank:]
    w_in_x = jnp.concatenate([w_in[:, :rank], w_kr, _rotate_half_cols(w_kr)], axis=1).astype(BF16)
    wq = w_uq.reshape(MLA_Q_RANK, MLA_HEADS, MLA_NOPE_DIM + MLA_ROPE_DIM)
    wq_rope = wq[..., MLA_NOPE_DIM:]
    w_uq_x = jnp.concatenate([wq[..., :MLA_NOPE_DIM], wq_rope, _rotate_half_cols(wq_rope)], axis=-1)
    w_uq_x = w_uq_x.reshape(MLA_Q_RANK, MLA_HEADS * MLA_QK_DIM).astype(BF16)
    wkv = w_ukv.reshape(MLA_KV_RANK, MLA_HEADS, MLA_NOPE_DIM + MLA_V_DIM)
    w_ukv_x = jnp.concatenate([wkv[..., :MLA_NOPE_DIM].reshape(MLA_KV_RANK, -1),
                               wkv[..., MLA_NOPE_DIM:].reshape(MLA_KV_RANK, -1)], axis=1).astype(BF16)
    half = MLA_ROPE_DIM // 2
    inv_freq = ROPE_THETA ** (-jnp.arange(half, dtype=F32) / half)
    pos = jnp.arange(lp) - FIRST_VALID
    ang = pos.astype(F32)[:, None] * inv_freq[None, :]
    zeros = jnp.zeros((lp, LANES - MLA_ROPE_DIM), F32)
    cos = jnp.concatenate([jnp.cos(ang), jnp.cos(ang), zeros], axis=1)
    sin = jnp.concatenate([jnp.sin(ang), jnp.sin(ang), zeros], axis=1)
    q, k, v = _mla_proj(h2, g_pre, w_in_x, q_norm[None], kv_norm[None], w_uq_x, w_ukv_x, cos, sin, tm, lp)
    vt = jnp.transpose(v.reshape(b, lp // ATT_BLOCK, ATT_BLOCK, MLA_HEADS, MLA_V_DIM), (0, 3, 1, 4, 2))
    o = _mla_attn(q.reshape(b, lp, -1), k.reshape(b, lp, -1), vt)
    return _out_proj(o.reshape(b * lp, -1), h2, g_post, w_out.astype(BF16), tm, lp)


def _head_sum(x, bd_ref):
    return _dot(x.astype(BF16), bd_ref[...])


def _rw_proj_body(x_ref, xp_ref, g_ref, mix_ref, wrkv_ref, w0_ref, w1_ref, w2_ref, a0_ref, a1_ref, a2_ref,
                  g1_ref, g2_ref, kk_ref, ka_ref, rk_ref, bd_ref,
                  r_ref, lw_ref, k_ref, v_ref, na_ref, b_ref, gate_ref, bonus_ref, *, lp):
    tm = x_ref.shape[0]
    g = g_ref[...]
    u = _rms(x_ref[...], g)
    first = (pl.program_id(0) * tm) % lp == 0
    u_prev = jnp.where(first, 0.0, _rms(xp_ref[...], g)[7:8, :])
    shifted = jnp.where(lax.broadcasted_iota(jnp.int32, (tm, 1), 0) == 0, u_prev, pltpu.roll(u, 1, 0))
    xx = shifted - u
    mix = mix_ref[...]
    xs = lambda n: (u + xx * mix[n:n + 1, :]).astype(BF16)
    r = _dot(xs(0), wrkv_ref[0])
    k = _dot(xs(1), wrkv_ref[1])
    v = _dot(xs(2), wrkv_ref[2])
    wl = w0_ref[...] + _dot(jnp.tanh(_dot(xs(3), w1_ref[...])).astype(BF16), w2_ref[...])
    w_log = -(jnp.maximum(-wl, 0.0) + jnp.log(1.0 + jnp.exp(-jnp.abs(wl)))) - 0.5
    lw_ref[...] = -jnp.exp(w_log)
    a = jax.nn.sigmoid(a0_ref[...] + _dot(_dot(xs(4), a1_ref[...]).astype(BF16), a2_ref[...]))
    gate = _dot(jax.nn.sigmoid(_dot(xs(5), g1_ref[...])).astype(BF16), g2_ref[...])
    gate_ref[...] = gate.astype(gate_ref.dtype)
    kk = k * kk_ref[...]
    kk = kk / jnp.maximum(jnp.sqrt(_head_sum(kk * kk, bd_ref)), 1e-12)
    k2 = k * (1.0 + (a - 1.0) * ka_ref[...])
    r_ref[...] = r.astype(r_ref.dtype)
    k_ref[...] = k2.astype(k_ref.dtype)
    v_ref[...] = v.astype(v_ref.dtype)
    na_ref[...] = (-kk).astype(na_ref.dtype)
    b_ref[...] = (kk * a).astype(b_ref.dtype)
    bonus_ref[...] = _head_sum(r * k2 * rk_ref[...], bd_ref) * v


def _rw_proj(h2, g_pre, mix, w_rkv, w0, w1, w2, a0, a1, a2, g1, g2, k_k, k_a, r_k, bd, tm, lp):
    n, d = h2.shape
    row = pl.BlockSpec((tm, d), lambda i: (i, 0))
    prev = pl.BlockSpec((8, d), lambda i: (jnp.maximum(i * (tm // 8) - 1, 0), 0))
    vec = _resident((1, d))
    outs = [jax.ShapeDtypeStruct((n, d), dt) for dt in (BF16, F32, BF16, BF16, BF16, BF16, BF16, F32)]
    return pl.pallas_call(
        functools.partial(_rw_proj_body, lp=lp),
        grid=(n // tm,),
        in_specs=[row, prev, vec, _resident(mix.shape), _resident(w_rkv.shape), vec, _resident(w1.shape),
                  _resident(w2.shape), vec, _resident(a1.shape), _resident(a2.shape), _resident(g1.shape),
                  _resident(g2.shape), vec, vec, vec, _resident(bd.shape)],
        out_specs=[row] * 8,
        out_shape=outs,
        compiler_params=_params("parallel"),
        name="rw_proj",
    )(h2, h2, g_pre, mix, w_rkv, w0, w1, w2, a0, a1, a2, g1, g2, k_k, k_a, r_k, bd)


def _rw_scan_body(r_ref, lw_ref, k_ref, v_ref, a_ref, b_ref, y_ref, s_ref):
    c = RW_CHUNK
    gw = RW_GROUP
    heads = gw // c

    @pl.when(pl.program_id(2) == 0)
    def _():
        s_ref[...] = jnp.zeros_like(s_ref)

    ri = lax.broadcasted_iota(jnp.int32, (gw, gw), 0)
    ci = lax.broadcasted_iota(jnp.int32, (gw, gw), 1)
    shift = int(math.log2(c))
    block_mask = (ri >> shift) == (ci >> shift)
    t_idx = lax.broadcasted_iota(jnp.int32, (c, gw), 0)
    s_idx = lax.broadcasted_iota(jnp.int32, (c, gw), 1) & (c - 1)
    strict = s_idx < t_idx
    incl = s_idx <= t_idx
    eye = (s_idx == t_idx).astype(F32)
    tri = (lax.broadcasted_iota(jnp.int32, (c, c), 1) <= lax.broadcasted_iota(jnp.int32, (c, c), 0)).astype(BF16)

    def bd(x):
        return jnp.where(block_mask, jnp.concatenate([x.astype(BF16)] * heads, axis=0), 0.0)

    def lower(x, mask):
        return jnp.where(mask, x, 0.0).astype(BF16)

    n_chunks = r_ref.shape[1] // c
    pre = []
    for j in range(n_chunks):
        rows = slice(j * c, (j + 1) * c)
        r, lw, k, v, a, b = (ref[0, rows, :] for ref in (r_ref, lw_ref, k_ref, v_ref, a_ref, b_ref))
        cum = _cumsum_rows(lw, tri)
        total = cum[c - 1:c, :]
        lhs = jnp.concatenate([a * jnp.exp(cum - lw), r * jnp.exp(cum)], axis=0).astype(BF16)
        g_inv = jnp.exp(-cum)
        a_b = _dot_nt(lhs, bd(b * g_inv))
        a_k = _dot_nt(lhs, bd(k * g_inv))
        g_out = jnp.exp(total - cum)
        rhs_out = jnp.concatenate([b * g_out, k * g_out], axis=0).astype(BF16)
        pre.append(dict(lhs=lhs, v=v, total=total, rhs_out=rhs_out, n=jnp.where(strict, a_b[:c], 0.0),
                        a_ak=lower(a_k[:c], strict), a_rb=lower(a_b[c:], incl), a_rk=lower(a_k[c:], incl)))
    t_mats = [eye + p["n"] for p in pre]
    pws = [p["n"] for p in pre]
    for _ in range(int(math.log2(c)) - 1):
        pws = [_dot(pw.astype(BF16), bd(pw)) for pw in pws]
        t_mats = [t + _dot(t.astype(BF16), bd(pw)) for t, pw in zip(t_mats, pws)]
    affine = []
    for j in range(n_chunks):
        p = pre[j]
        t16 = t_mats[j].astype(BF16)
        bd_v = bd(p["v"])
        w1 = _dot(t16, bd(p["lhs"][:c]))
        u0 = _dot(t16, bd(_dot(p["a_ak"], bd_v)))
        r2 = (p["lhs"][c:].astype(F32) + _dot(p["a_rb"], bd(w1))).astype(BF16)
        y0 = _dot(p["a_rb"], bd(u0)) + _dot(p["a_rk"], bd_v)
        p_mat = jnp.where(block_mask, _dot_tn(w1.astype(BF16), p["rhs_out"][:c]), 0.0).astype(BF16)
        n0 = jnp.where(block_mask, _dot_tn(jnp.concatenate([u0, p["v"]], axis=0).astype(BF16), p["rhs_out"]), 0.0)
        affine.append((r2, y0, p_mat, n0, jnp.exp(p["total"])))
    s_t = s_ref[...]
    for j in range(n_chunks):
        r2, y0, p_mat, n0, decay = affine[j]
        s16 = s_t.astype(BF16)
        y_ref[0, j * c:(j + 1) * c, :] = _dot_nt(r2, s16) + y0
        s_t = s_t * decay + _dot(s16, p_mat) + n0
    s_ref[...] = s_t


def _cumsum_rows(x, tri):
    hi, lo = _split(x)
    return _dot(tri, hi) + _dot(tri, lo)


def _rw_scan(r, lw, k, v, a, b):
    bsz, lp, d = r.shape
    rows = RW_CHUNK * next(n for n in RW_CHUNKS_PER_STEP if lp % (RW_CHUNK * n) == 0)
    spec = pl.BlockSpec((1, rows, RW_GROUP), lambda bi, g, i: (bi, i, g))
    return pl.pallas_call(
        _rw_scan_body,
        grid=(bsz, d // RW_GROUP, lp // rows),
        in_specs=[spec] * 6,
        out_specs=spec,
        out_shape=jax.ShapeDtypeStruct((bsz, lp, d), F32),
        scratch_shapes=[pltpu.VMEM((RW_GROUP, RW_GROUP), F32)],
        compiler_params=_params("parallel", "parallel", "arbitrary"),
        name="rw_scan",
    )(r, lw, k, v, a, b)


def _rw_out_body(y_ref, bonus_ref, gate_ref, h_ref, lnw_ref, lnb_ref, g_ref, w_ref, bd_ref, out_ref, *, lp):
    tm = h_ref.shape[0]
    y = y_ref[...]
    mu = _head_sum(y, bd_ref) * (1.0 / RW_HEAD_DIM)
    dlt = y - mu
    var = _head_sum(dlt * dlt, bd_ref) * (1.0 / RW_HEAD_DIM)
    yn = dlt * lax.rsqrt(var + RW_LN_EPS) * lnw_ref[...] + lnb_ref[...]
    z = ((yn + bonus_ref[...]) * gate_ref[...]).astype(BF16)
    m = _dot(z, w_ref[...])
    upd = jnp.where(_row_valid(tm, lp), _rms(m, g_ref[...]), 0.0)
    out_ref[...] = h_ref[...] + upd


def _rw_out(y2, bonus, gate, h2, ln_w, ln_b, g_post, w_out, bd, tm, lp):
    n, d = h2.shape
    row = pl.BlockSpec((tm, d), lambda i: (i, 0))
    vec = _resident((1, d))
    return pl.pallas_call(
        functools.partial(_rw_out_body, lp=lp),
        grid=(n // tm,),
        in_specs=[row, row, row, row, vec, vec, vec, _resident(w_out.shape), _resident(bd.shape)],
        out_specs=row,
        out_shape=jax.ShapeDtypeStruct((n, d), F32),
        compiler_params=_params("parallel"),
        name="rw_out",
    )(y2, bonus, gate, h2, ln_w, ln_b, g_post, w_out, bd)


def _pad_cols(w, width):
    return jnp.pad(w, ((0, 0), (0, width - w.shape[1])))


def _pad_rows(w, height):
    return jnp.pad(w, ((0, height - w.shape[0]), (0, 0)))


def _rwkv7_mixer(h2, g_pre, g_post, mix, w_in, w0, w1, w2, a0, a1, a2, g1, g2, k_k, k_a, r_k, ln_w, ln_b, w_out,
                 b, lp, tm):
    d = h2.shape[1]
    idx = jnp.arange(d) // RW_HEAD_DIM
    bd = (idx[:, None] == idx[None, :]).astype(BF16)
    lora = lambda w_a, w_b: (_pad_cols(w_a, -(-w_a.shape[1] // LANES) * LANES).astype(BF16),
                             _pad_rows(w_b, -(-w_b.shape[0] // LANES) * LANES).astype(BF16))
    w1p, w2p = lora(w1, w2)
    a1p, a2p = lora(a1, a2)
    g1p, g2p = lora(g1, g2)
    outs = _rw_proj(h2, g_pre, mix, w_in.astype(BF16), w0[None], w1p, w2p, a0[None], a1p, a2p, g1p, g2p,
                    k_k[None], k_a[None], r_k.reshape(1, d), bd, RW_PROJ_ROWS, lp)
    r, lw, k, v, na, bb, gate, bonus = outs
    y = _rw_scan(*(t.reshape(b, lp, d) for t in (r, lw, k, v, na, bb)))
    return _rw_out(y.reshape(b * lp, d), bonus, gate, h2, ln_w[None], ln_b[None], g_post, w_out.astype(BF16), bd, tm, lp)


def kernel(x, meta_tokens, norm_gains, ffn_w_in, ffn_w_out, sb_w_in, sb_w_out, mla_w_in, mla_q_norm, mla_kv_norm, mla_w_uq, mla_w_ukv, mla_w_out, rw_mix, rw_w_in, rw_w0, rw_w1, rw_w2, rw_a0, rw_a1, rw_a2, rw_g1, rw_g2, rw_k_k, rw_k_a, rw_r_k, rw_ln_w, rw_ln_b, rw_w_out):
    b, seq, d = x.shape
    assert d == D_MODEL and seq % ATT_BLOCK == 0
    lp = PAD + seq
    tm = _row_tile(lp)
    meta = jnp.broadcast_to(meta_tokens.astype(x.dtype)[None], (b, N_META, d))
    h = jnp.concatenate([jnp.zeros((b, FIRST_VALID, d), x.dtype), meta, x], axis=1)
    h2 = h.reshape(b * lp, d)
    depth = norm_gains.shape[0]
    for i in range(depth):
        g = norm_gains[i][:, :, None, :]
        h2 = _ffn(h2, g[0, 0], g[0, 1], ffn_w_in[i, 0].astype(BF16), ffn_w_out[i, 0].astype(BF16), tm)
        kind, slot = i % 3, i // 3
        if kind == 0:
            h2 = _stick_breaking_mixer(h2, g[1, 0], g[1, 1], sb_w_in[slot], sb_w_out[slot], b, lp, tm)
        elif kind == 1:
            h2 = _mla_mixer(h2, g[1, 0], g[1, 1], mla_w_in[slot], mla_q_norm[slot], mla_kv_norm[slot],
                            mla_w_uq[slot], mla_w_ukv[slot], mla_w_out[slot], b, lp, tm)
        else:
            h2 = _rwkv7_mixer(h2, g[1, 0], g[1, 1], rw_mix[slot], rw_w_in[slot], rw_w0[slot], rw_w1[slot],
                              rw_w2[slot], rw_a0[slot], rw_a1[slot], rw_a2[slot], rw_g1[slot], rw_g2[slot],
                              rw_k_k[slot], rw_k_a[slot], rw_r_k[slot], rw_ln_w[slot], rw_ln_b[slot],
                              rw_w_out[slot], b, lp, tm)
        h2 = _ffn(h2, g[2, 0], g[2, 1], ffn_w_in[i, 1].astype(BF16), ffn_w_out[i, 1].astype(BF16), tm)
    return h2.reshape(b, lp, d)[:, PAD:]
```

```python
import functools
import math

import jax
import jax.numpy as jnp
from jax import lax
from jax.experimental import pallas as pl
from jax.experimental.pallas import tpu as pltpu

F32 = jnp.float32
BF16 = jnp.bfloat16

D_MODEL = 1024
N_META = 16
RMS_EPS = 1e-6
SB_HEADS = 8
SB_HEAD_DIM = D_MODEL // SB_HEADS
SB_SCALE = 1.0 / math.sqrt(SB_HEAD_DIM)
MLA_HEADS = 8
MLA_NOPE_DIM = 128
MLA_ROPE_DIM = 64
MLA_V_DIM = 128
MLA_Q_RANK = 3 * D_MODEL // 8
MLA_KV_RANK = D_MODEL // 4
MLA_SCALE = 1.0 / math.sqrt(MLA_NOPE_DIM + MLA_ROPE_DIM)
ROPE_THETA = 10000.0
RW_HEAD_DIM = 64
RW_LN_EPS = 64e-5
D_FF = ((8 * D_MODEL // 3 + 127) // 128) * 128

LANES = 128
MXU_DIM = 256
PAD = 256
FIRST_VALID = PAD - N_META
ATT_BLOCK = MXU_DIM
FFN_CHUNK = MXU_DIM
RW_CHUNK = RW_HEAD_DIM
RW_GROUP = MXU_DIM
RW_CHUNKS_PER_STEP = (10, 5, 4, 2, 1)
RW_PROJ_ROWS = 256
MASK_VALUE = -1e30
SB_UNDERFLOW_LOG = 110.0
VMEM_LIMIT = 56 * 1024 * 1024


def _row_tile(lp):
    for t in (640, 512, 256, 128):
        if lp % t == 0:
            return t
    raise ValueError(f"unsupported padded length {lp}")


def _params(*sem, flags=None):
    return pltpu.CompilerParams(dimension_semantics=sem, vmem_limit_bytes=VMEM_LIMIT, flags=flags)


def _resident(shape):
    nd = len(shape)
    return pl.BlockSpec(shape, lambda *_: (0,) * nd, pipeline_mode=pl.Buffered(1))


def _rms(x, g):
    return x * lax.rsqrt(jnp.mean(x * x, axis=-1, keepdims=True) + RMS_EPS) * g


def _dot(a, b):
    return jnp.dot(a, b, preferred_element_type=F32)


def _dot_nt(a, b):
    return lax.dot_general(a, b, (((1,), (1,)), ((), ())), preferred_element_type=F32)


def _dot_tn(a, b):
    return lax.dot_general(a, b, (((0,), (0,)), ((), ())), preferred_element_type=F32)


def _split(x):
    hi = x.astype(BF16)
    lo = (x - hi.astype(F32)).astype(BF16)
    return hi, lo


def _row_valid(tile_rows, lp):
    pos0 = (pl.program_id(0) * tile_rows) % lp
    pos = pos0 + lax.broadcasted_iota(jnp.int32, (tile_rows, 1), 0)
    return pos >= FIRST_VALID


def _ffn_body(x_ref, gpre_ref, gpost_ref, win_ref, wout_ref, o_ref):
    x = x_ref[...]
    xn = _rms(x, gpre_ref[...]).astype(BF16)
    acc = jnp.zeros(x.shape, F32)
    for c in range(D_FF // FFN_CHUNK):
        lo = c * FFN_CHUNK
        gate = _dot(xn, win_ref[:, lo:lo + FFN_CHUNK])
        up = _dot(xn, win_ref[:, D_FF + lo:D_FF + lo + FFN_CHUNK])
        act = (gate * jax.nn.sigmoid(gate) * up).astype(BF16)
        acc = acc + _dot(act, wout_ref[lo:lo + FFN_CHUNK, :])
    o_ref[...] = x + 0.5 * _rms(acc, gpost_ref[...])


def _ffn(h2, g_pre, g_post, w_in, w_out, tm):
    n, d = h2.shape
    row = pl.BlockSpec((tm, d), lambda i: (i, 0))
    return pl.pallas_call(
        _ffn_body,
        grid=(n // tm,),
        in_specs=[row, _resident((1, d)), _resident((1, d)), _resident(w_in.shape), _resident(w_out.shape)],
        out_specs=row,
        out_shape=jax.ShapeDtypeStruct((n, d), F32),
        compiler_params=_params("parallel"),
        name="ffn",
    )(h2, g_pre, g_post, w_in, w_out)


def _out_proj_body(o_ref, h_ref, g_ref, w_ref, out_ref, *, lp):
    tm = h_ref.shape[0]
    m = _dot(o_ref[...], w_ref[...])
    upd = jnp.where(_row_valid(tm, lp), _rms(m, g_ref[...]), 0.0)
    out_ref[...] = h_ref[...] + upd


def _out_proj(o2, h2, g_post, w_out, tm, lp):
    n, d = h2.shape
    return pl.pallas_call(
        functools.partial(_out_proj_body, lp=lp),
        grid=(n // tm,),
        in_specs=[pl.BlockSpec((tm, o2.shape[1]), lambda i: (i, 0)), pl.BlockSpec((tm, d), lambda i: (i, 0)),
                  _resident((1, d)), _resident(w_out.shape)],
        out_specs=pl.BlockSpec((tm, d), lambda i: (i, 0)),
        out_shape=jax.ShapeDtypeStruct((n, d), F32),
        compiler_params=_params("parallel"),
        name="out_proj",
    )(o2, h2, g_post, w_out)


def _sb_proj_body(x_ref, g_ref, w_ref, qkv_ref):
    u = _rms(x_ref[...], g_ref[...]).astype(BF16)
    qkv = _dot(u, w_ref[...])
    d = x_ref.shape[1]
    qkv_ref[:, :d] = (qkv[:, :d] * SB_SCALE).astype(BF16)
    qkv_ref[:, d:] = qkv[:, d:].astype(BF16)


def _sb_proj(h2, g_pre, w_in, tm):
    n, d = h2.shape
    return pl.pallas_call(
        _sb_proj_body,
        grid=(n // tm,),
        in_specs=[pl.BlockSpec((tm, d), lambda i: (i, 0)), _resident((1, d)), _resident(w_in.shape)],
        out_specs=pl.BlockSpec((tm, w_in.shape[1]), lambda i: (i, 0)),
        out_shape=jax.ShapeDtypeStruct((n, w_in.shape[1]), BF16),
        compiler_params=_params("parallel"),
        name="sb_proj",
    )(h2, g_pre, w_in)


def _walk_key_steps(top, front, back, state):
    def body(_, carried):
        step, fr, st = carried
        nxt = front(step - 1, False)
        return step - 1, nxt, back(step, fr, st)

    _, fr, st = lax.fori_loop(0, top, body, (top, front(top, True), state))
    return back(0, fr, st)


def _reduce_rows(x, op, group=64):
    parts = [x[i:i + group] for i in range(0, x.shape[0], group)]
    while len(parts) > 1:
        parts = [op(parts[i], parts[i + 1]) for i in range(0, len(parts) - 1, 2)] + parts[len(parts) & ~1:]
    part = parts[0]
    return jnp.max(part, axis=0, keepdims=True) if op is jnp.maximum else jnp.sum(part, axis=0, keepdims=True)


def _lane_tile(x, reps):
    return x if reps == 1 else jnp.concatenate([x] * reps, axis=1)


def _sb_attn_body(q_ref, k_ref, v_ref, tri_ref, o_ref):
    tb = ATT_BLOCK
    n = q_ref.shape[1] // tb
    qi = pl.program_id(2)
    tri = tri_ref[...]
    sign = jnp.int32(-2 ** 31)
    last = FIRST_VALID // tb

    def step(c, kb, masked, carry, acc):
        k0 = pl.multiple_of(jnp.maximum(kb, 0) * tb, tb)
        k = k_ref[0, pl.ds(k0, tb), :]
        v = v_ref[0, pl.ds(k0, tb), :]
        z = _dot_nt(q_ref[0, c * tb:(c + 1) * tb, :], k)
        neg_abs = lax.bitcast_convert_type(lax.bitcast_convert_type(z, jnp.int32) | sign, F32)
        sp = jnp.maximum(z, 0.0) + jnp.log(1.0 + jnp.exp(neg_abs))
        if masked:
            row = (qi * n + c) * tb + lax.broadcasted_iota(jnp.int32, (tb, tb), 0)
            col = k0 + lax.broadcasted_iota(jnp.int32, (tb, tb), 1)
            valid = (col < row) & (col >= FIRST_VALID) & (kb >= last)
            sp = jnp.where(valid, sp, 0.0)
        within = _dot(sp.astype(BF16), tri)
        p = jnp.exp(z - within - _lane_tile(carry, tb // LANES))
        if masked:
            p = jnp.where(valid, p, 0.0)
        return carry + jnp.broadcast_to(within[:, 0:1], carry.shape), acc + _dot(p.astype(BF16), v)

    def round_(j, masked, state):
        return tuple(step(c, qi * n + c - j, masked, *state[c]) for c in range(n))

    def live(state):
        carry = functools.reduce(jnp.minimum, [s[0] for s in state])
        return jnp.min(carry) < SB_UNDERFLOW_LOG

    zero = (jnp.zeros((tb, LANES), F32), jnp.zeros((tb, SB_HEAD_DIM), F32))
    state = round_(0, True, (zero,) * n)
    j, state = lax.while_loop(lambda c: (qi * n - c[0] > last) & live(c[1]),
                              lambda c: (c[0] + 1, round_(c[0], False, c[1])), (1, state))
    j, state = lax.while_loop(lambda c: (qi * n + (n - 1) - c[0] >= last) & live(c[1]),
                              lambda c: (c[0] + 1, round_(c[0], True, c[1])), (j, state))
    for c in range(n):
        o_ref[0, c * tb:(c + 1) * tb, :] = state[c][1].astype(o_ref.dtype)


def _sb_chains(lp):
    return next(n for n in (5, 4, 3, 2, 1) if (lp // ATT_BLOCK) % n == 0)


def _sb_attn(qkv, tri):
    b, lp, _ = qkv.shape
    tb = ATT_BLOCK * _sb_chains(lp)
    hd = SB_HEAD_DIM
    return pl.pallas_call(
        _sb_attn_body,
        grid=(b, SB_HEADS, lp // tb),
        in_specs=[pl.BlockSpec((1, tb, hd), lambda bi, h, i: (bi, i, h)),
                  pl.BlockSpec((1, lp, hd), lambda bi, h, i: (bi, 0, SB_HEADS + h)),
                  pl.BlockSpec((1, lp, hd), lambda bi, h, i: (bi, 0, 2 * SB_HEADS + h)),
                  _resident(tri.shape)],
        out_specs=pl.BlockSpec((1, tb, hd), lambda bi, h, i: (bi, i, h)),
        out_shape=jax.ShapeDtypeStruct((b, lp, SB_HEADS * hd), BF16),
        compiler_params=_params("parallel", "parallel", "arbitrary"),
        name="sb_attn",
    )(qkv, qkv, qkv, tri)


def _stick_breaking_mixer(h2, g_pre, g_post, w_in, w_out, b, lp, tm):
    qkv = _sb_proj(h2, g_pre, w_in.astype(BF16), tm)
    idx = jnp.arange(ATT_BLOCK)
    tri = (idx[:, None] >= idx[None, :]).astype(BF16)
    o = _sb_attn(qkv.reshape(b, lp, -1), tri)
    return _out_proj(o.reshape(b * lp, -1), h2, g_post, w_out.astype(BF16), tm, lp)


MLA_QK_DIM = MXU_DIM
MLA_Q_SCALE = MLA_SCALE * math.log2(math.e)
MLA_KEY_BLOCKS = 4
MLA_Q_BLOCK = 1 * ATT_BLOCK


def _rope_tile(t, cos, sin):
    return t * cos + pltpu.roll(t, MLA_ROPE_DIM, 1) * sin


def _mla_proj_body(x_ref, g_ref, win_ref, qn_ref, kvn_ref, wuq_ref, wukv_ref, cos_ref, sin_ref, q_ref, k_ref, v_ref,
                   *, lp):
    u = _rms(x_ref[...], g_ref[...]).astype(BF16)
    c = _dot(u, win_ref[...])
    cq = _rms(c[:, :MLA_Q_RANK], qn_ref[...]).astype(BF16)
    ckv = _rms(c[:, MLA_Q_RANK:MLA_Q_RANK + MLA_KV_RANK], kvn_ref[...]).astype(BF16)
    cos = cos_ref[...]
    sin = sin_ref[...]
    tm = x_ref.shape[0]
    bias_lane = lax.broadcasted_iota(jnp.int32, (tm, LANES), 1) == MLA_ROPE_DIM
    q_bias = jnp.where(bias_lane, 1.0, 0.0)
    k_bias = jnp.where(bias_lane & jnp.logical_not(_row_valid(tm, lp)), MASK_VALUE, 0.0)
    k_rope = (_rope_tile(c[:, MLA_Q_RANK + MLA_KV_RANK:], cos, sin) + k_bias).astype(BF16)
    q = _dot(cq, wuq_ref[...])
    kv = _dot(ckv, wukv_ref[...])
    for h in range(MLA_HEADS):
        lo = h * MLA_QK_DIM
        q_ref[:, lo:lo + LANES] = (q[:, lo:lo + LANES] * MLA_Q_SCALE).astype(BF16)
        q_rope = _rope_tile(q[:, lo + LANES:lo + 2 * LANES], cos, sin)
        q_ref[:, lo + LANES:lo + 2 * LANES] = (q_rope * MLA_Q_SCALE + q_bias).astype(BF16)
        k_ref[:, lo:lo + LANES] = kv[:, h * LANES:(h + 1) * LANES].astype(BF16)
        k_ref[:, lo + LANES:lo + 2 * LANES] = k_rope
    v_ref[...] = kv[:, MLA_HEADS * MLA_NOPE_DIM:].astype(BF16)


def _mla_proj(h2, g_pre, w_in, q_norm, kv_norm, w_uq, w_ukv, cos, sin, tm, lp):
    n, d = h2.shape
    tiles_per_seq = lp // tm
    row = lambda w: pl.BlockSpec((tm, w), lambda i: (i, 0))
    table = pl.BlockSpec((tm, LANES), lambda i: (i % tiles_per_seq, 0))
    qk_w = MLA_HEADS * MLA_QK_DIM
    v_w = MLA_HEADS * MLA_V_DIM
    return pl.pallas_call(
        functools.partial(_mla_proj_body, lp=lp),
        grid=(n // tm,),
        in_specs=[row(d), _resident((1, d)), _resident(w_in.shape), _resident(q_norm.shape), _resident(kv_norm.shape),
                  _resident(w_uq.shape), _resident(w_ukv.shape), table, table],
        out_specs=[row(qk_w), row(qk_w), row(v_w)],
        out_shape=[jax.ShapeDtypeStruct((n, qk_w), BF16), jax.ShapeDtypeStruct((n, qk_w), BF16),
                   jax.ShapeDtypeStruct((n, v_w), BF16)],
        compiler_params=_params("parallel"),
        name="mla_proj",
    )(h2, g_pre, w_in, q_norm, kv_norm, w_uq, w_ukv, cos, sin)


def _mla_attn_body(q_ref, k_ref, vt_ref, o_ref):
    tb = ATT_BLOCK
    per_step = MLA_KEY_BLOCKS
    tk = per_step * tb
    tq = q_ref.shape[1]
    lp = k_ref.shape[1]
    qi = pl.program_id(2)
    q = q_ref[0]

    def first_block(step):
        return jnp.minimum(step * per_step, lp // tb - per_step)

    def front(step, masked):
        kb0 = first_block(step)
        z = _dot_nt(k_ref[0, pl.ds(pl.multiple_of(kb0 * tb, tb), tk), :], q)
        if masked:
            key = kb0 * tb + lax.broadcasted_iota(jnp.int32, (tk, tq), 0)
            qry = qi * tq + lax.broadcasted_iota(jnp.int32, (tk, tq), 1)
            z = jnp.where((key <= qry) & (key >= jnp.maximum(step * tk, FIRST_VALID)), z, MASK_VALUE)
        return z, _reduce_rows(z, jnp.maximum)

    def back(step, fr, state):
        z, z_max = fr
        m_prev, l_prev, acc = state
        kb0 = first_block(step)
        m_new = jnp.maximum(m_prev, z_max)
        p = jnp.exp2(z - m_new)
        alpha = jnp.exp2(m_prev - m_new)
        l_new = alpha * l_prev + _reduce_rows(p, jnp.add)
        vt = jnp.concatenate([vt_ref[0, 0, kb0 + i] for i in range(per_step)], axis=1)
        return m_new, l_new, alpha * acc + _dot(vt, p.astype(BF16))

    init = (jnp.full((1, tq), MASK_VALUE, F32), jnp.zeros((1, tq), F32), jnp.zeros((MLA_V_DIM, tq), F32))
    top = lax.shift_right_logical(qi * tq + (tq - 1), int(math.log2(tk)))
    _, l, acc = _walk_key_steps(top, front, back, init)
    o_ref[0] = (acc / l).T.astype(o_ref.dtype)


def _mla_attn(q, k, vt):
    b, lp, _ = q.shape
    tb = ATT_BLOCK
    tq = MLA_Q_BLOCK
    assert lp % tq == 0 and lp >= MLA_KEY_BLOCKS * tb and FIRST_VALID < MLA_KEY_BLOCKS * tb
    return pl.pallas_call(
        _mla_attn_body,
        grid=(b, MLA_HEADS, lp // tq),
        in_specs=[pl.BlockSpec((1, tq, MLA_QK_DIM), lambda bi, h, i: (bi, i, h)),
                  pl.BlockSpec((1, lp, MLA_QK_DIM), lambda bi, h, i: (bi, 0, h)),
                  pl.BlockSpec((1, 1, lp // tb, MLA_V_DIM, tb), lambda bi, h, i: (bi, h, 0, 0, 0))],
        out_specs=pl.BlockSpec((1, tq, MLA_V_DIM), lambda bi, h, i: (bi, i, h)),
        out_shape=jax.ShapeDtypeStruct((b, lp, MLA_HEADS * MLA_V_DIM), BF16),
        compiler_params=_params("parallel", "parallel", "arbitrary"),
        name="mla_attn",
    )(q, k, vt)


def _rotate_half_cols(w):
    half = MLA_ROPE_DIM // 2
    return jnp.concatenate([-w[..., half:], w[..., :half]], axis=-1)


def _mla_mixer(h2, g_pre, g_post, w_in, q_norm, kv_norm, w_uq, w_ukv, w_out, b, lp, tm):
    rank = MLA_Q_RANK + MLA_KV_RANK
    w_kr = w_in[:, rank:]
    w_in_x = jnp.concatenate([w_in[:, :rank], w_kr, _rotate_half_cols(w_kr)], axis=1).astype(BF16)
    wq = w_uq.reshape(MLA_Q_RANK, MLA_HEADS, MLA_NOPE_DIM + MLA_ROPE_DIM)
    wq_rope = wq[..., MLA_NOPE_DIM:]
    w_uq_x = jnp.concatenate([wq[..., :MLA_NOPE_DIM], wq_rope, _rotate_half_cols(wq_rope)], axis=-1)
    w_uq_x = w_uq_x.reshape(MLA_Q_RANK, MLA_HEADS * MLA_QK_DIM).astype(BF16)
    wkv = w_ukv.reshape(MLA_KV_RANK, MLA_HEADS, MLA_NOPE_DIM + MLA_V_DIM)
    w_ukv_x = jnp.concatenate([wkv[..., :MLA_NOPE_DIM].reshape(MLA_KV_RANK, -1),
                               wkv[..., MLA_NOPE_DIM:].reshape(MLA_KV_RANK, -1)], axis=1).astype(BF16)
    half = MLA_ROPE_DIM // 2
    inv_freq = ROPE_THETA ** (-jnp.arange(half, dtype=F32) / half)
    pos = jnp.arange(lp) - FIRST_VALID
    ang = pos.astype(F32)[:, None] * inv_freq[None, :]
    zeros = jnp.zeros((lp, LANES - MLA_ROPE_DIM), F32)
    cos = jnp.concatenate([jnp.cos(ang), jnp.cos(ang), zeros], axis=1)
    sin = jnp.concatenate([jnp.sin(ang), jnp.sin(ang), zeros], axis=1)
    q, k, v = _mla_proj(h2, g_pre, w_in_x, q_norm[None], kv_norm[None], w_uq_x, w_ukv_x, cos, sin, tm, lp)
    vt = jnp.transpose(v.reshape(b, lp // ATT_BLOCK, ATT_BLOCK, MLA_HEADS, MLA_V_DIM), (0, 3, 1, 4, 2))
    o = _mla_attn(q.reshape(b, lp, -1), k.reshape(b, lp, -1), vt)
    return _out_proj(o.reshape(b * lp, -1), h2, g_post, w_out.astype(BF16), tm, lp)


def _head_sum(x, bd_ref):
    return _dot(x.astype(BF16), bd_ref[...])


def _rw_proj_body(x_ref, xp_ref, g_ref, mix_ref, wrkv_ref, w0_ref, w1_ref, w2_ref, a0_ref, a1_ref, a2_ref,
                  g1_ref, g2_ref, kk_ref, ka_ref, rk_ref, bd_ref,
                  r_ref, lw_ref, k_ref, v_ref, na_ref, b_ref, gate_ref, bonus_ref, *, lp):
    tm = x_ref.shape[0]
    g = g_ref[...]
    u = _rms(x_ref[...], g)
    first = (pl.program_id(0) * tm) % lp == 0
    u_prev = jnp.where(first, 0.0, _rms(xp_ref[...], g)[7:8, :])
    shifted = jnp.where(lax.broadcasted_iota(jnp.int32, (tm, 1), 0) == 0, u_prev, pltpu.roll(u, 1, 0))
    xx = shifted - u
    mix = mix_ref[...]
    xs = lambda n: (u + xx * mix[n:n + 1, :]).astype(BF16)
    r = _dot(xs(0), wrkv_ref[0])
    k = _dot(xs(1), wrkv_ref[1])
    v = _dot(xs(2), wrkv_ref[2])
    wl = w0_ref[...] + _dot(jnp.tanh(_dot(xs(3), w1_ref[...])).astype(BF16), w2_ref[...])
    w_log = -(jnp.maximum(-wl, 0.0) + jnp.log(1.0 + jnp.exp(-jnp.abs(wl)))) - 0.5
    lw_ref[...] = -jnp.exp(w_log)
    a = jax.nn.sigmoid(a0_ref[...] + _dot(_dot(xs(4), a1_ref[...]).astype(BF16), a2_ref[...]))
    gate = _dot(jax.nn.sigmoid(_dot(xs(5), g1_ref[...])).astype(BF16), g2_ref[...])
    gate_ref[...] = gate.astype(gate_ref.dtype)
    kk = k * kk_ref[...]
    kk = kk / jnp.maximum(jnp.sqrt(_head_sum(kk * kk, bd_ref)), 1e-12)
    k2 = k * (1.0 + (a - 1.0) * ka_ref[...])
    r_ref[...] = r.astype(r_ref.dtype)
    k_ref[...] = k2.astype(k_ref.dtype)
    v_ref[...] = v.astype(v_ref.dtype)
    na_ref[...] = (-kk).astype(na_ref.dtype)
    b_ref[...] = (kk * a).astype(b_ref.dtype)
    bonus_ref[...] = _head_sum(r * k2 * rk_ref[...], bd_ref) * v


def _rw_proj(h2, g_pre, mix, w_rkv, w0, w1, w2, a0, a1, a2, g1, g2, k_k, k_a, r_k, bd, tm, lp):
    n, d = h2.shape
    row = pl.BlockSpec((tm, d), lambda i: (i, 0))
    prev = pl.BlockSpec((8, d), lambda i: (jnp.maximum(i * (tm // 8) - 1, 0), 0))
    vec = _resident((1, d))
    outs = [jax.ShapeDtypeStruct((n, d), dt) for dt in (BF16, F32, BF16, BF16, BF16, BF16, BF16, F32)]
    return pl.pallas_call(
        functools.partial(_rw_proj_body, lp=lp),
        grid=(n // tm,),
        in_specs=[row, prev, vec, _resident(mix.shape), _resident(w_rkv.shape), vec, _resident(w1.shape),
                  _resident(w2.shape), vec, _resident(a1.shape), _resident(a2.shape), _resident(g1.shape),
                  _resident(g2.shape), vec, vec, vec, _resident(bd.shape)],
        out_specs=[row] * 8,
        out_shape=outs,
        compiler_params=_params("parallel"),
        name="rw_proj",
    )(h2, h2, g_pre, mix, w_rkv, w0, w1, w2, a0, a1, a2, g1, g2, k_k, k_a, r_k, bd)


def _rw_scan_body(r_ref, lw_ref, k_ref, v_ref, a_ref, b_ref, y_ref, s_ref):
    c = RW_CHUNK
    gw = RW_GROUP
    heads = gw // c

    @pl.when(pl.program_id(2) == 0)
    def _():
        s_ref[...] = jnp.zeros_like(s_ref)

    ri = lax.broadcasted_iota(jnp.int32, (gw, gw), 0)
    ci = lax.broadcasted_iota(jnp.int32, (gw, gw), 1)
    shift = int(math.log2(c))
    block_mask = (ri >> shift) == (ci >> shift)
    t_idx = lax.broadcasted_iota(jnp.int32, (c, gw), 0)
    s_idx = lax.broadcasted_iota(jnp.int32, (c, gw), 1) & (c - 1)
    strict = s_idx < t_idx
    incl = s_idx <= t_idx
    eye = (s_idx == t_idx).astype(F32)
    tri = (lax.broadcasted_iota(jnp.int32, (c, c), 1) <= lax.broadcasted_iota(jnp.int32, (c, c), 0)).astype(BF16)

    def bd(x):
        return jnp.where(block_mask, jnp.concatenate([x.astype(BF16)] * heads, axis=0), 0.0)

    def lower(x, mask):
        return jnp.where(mask, x, 0.0).astype(BF16)

    n_chunks = r_ref.shape[1] // c
    pre = []
    for j in range(n_chunks):
        rows = slice(j * c, (j + 1) * c)
        r, lw, k, v, a, b = (ref[0, rows, :] for ref in (r_ref, lw_ref, k_ref, v_ref, a_ref, b_ref))
        cum = _cumsum_rows(lw, tri)
        total = cum[c - 1:c, :]
        lhs = jnp.concatenate([a * jnp.exp(cum - lw), r * jnp.exp(cum)], axis=0).astype(BF16)
        g_inv = jnp.exp(-cum)
        a_b = _dot_nt(lhs, bd(b * g_inv))
        a_k = _dot_nt(lhs, bd(k * g_inv))
        g_out = jnp.exp(total - cum)
        rhs_out = jnp.concatenate([b * g_out, k * g_out], axis=0).astype(BF16)
        pre.append(dict(lhs=lhs, v=v, total=total, rhs_out=rhs_out, n=jnp.where(strict, a_b[:c], 0.0),
                        a_ak=lower(a_k[:c], strict), a_rb=lower(a_b[c:], incl), a_rk=lower(a_k[c:], incl)))
    t_mats = [eye + p["n"] for p in pre]
    pws = [p["n"] for p in pre]
    for _ in range(int(math.log2(c)) - 1):
        pws = [_dot(pw.astype(BF16), bd(pw)) for pw in pws]
        t_mats = [t + _dot(t.astype(BF16), bd(pw)) for t, pw in zip(t_mats, pws)]
    affine = []
    for j in range(n_chunks):
        p = pre[j]
        t16 = t_mats[j].astype(BF16)
        bd_v = bd(p["v"])
        w1 = _dot(t16, bd(p["lhs"][:c]))
        u0 = _dot(t16, bd(_dot(p["a_ak"], bd_v)))
        r2 = (p["lhs"][c:].astype(F32) + _dot(p["a_rb"], bd(w1))).astype(BF16)
        y0 = _dot(p["a_rb"], bd(u0)) + _dot(p["a_rk"], bd_v)
        p_mat = jnp.where(block_mask, _dot_tn(w1.astype(BF16), p["rhs_out"][:c]), 0.0).astype(BF16)
        n0 = jnp.where(block_mask, _dot_tn(jnp.concatenate([u0, p["v"]], axis=0).astype(BF16), p["rhs_out"]), 0.0)
        affine.append((r2, y0, p_mat, n0, jnp.exp(p["total"])))
    s_t = s_ref[...]
    for j in range(n_chunks):
        r2, y0, p_mat, n0, decay = affine[j]
        s16 = s_t.astype(BF16)
        y_ref[0, j * c:(j + 1) * c, :] = _dot_nt(r2, s16) + y0
        s_t = s_t * decay + _dot(s16, p_mat) + n0
    s_ref[...] = s_t


def _cumsum_rows(x, tri):
    hi, lo = _split(x)
    return _dot(tri, hi) + _dot(tri, lo)


def _rw_scan(r, lw, k, v, a, b):
    bsz, lp, d = r.shape
    rows = RW_CHUNK * next(n for n in RW_CHUNKS_PER_STEP if lp % (RW_CHUNK * n) == 0)
    spec = pl.BlockSpec((1, rows, RW_GROUP), lambda bi, g, i: (bi, i, g))
    return pl.pallas_call(
        _rw_scan_body,
        grid=(bsz, d // RW_GROUP, lp // rows),
        in_specs=[spec] * 6,
        out_specs=spec,
        out_shape=jax.ShapeDtypeStruct((bsz, lp, d), F32),
        scratch_shapes=[pltpu.VMEM((RW_GROUP, RW_GROUP), F32)],
        compiler_params=_params("parallel", "parallel", "arbitrary"),
        name="rw_scan",
    )(r, lw, k, v, a, b)


def _rw_out_body(y_ref, bonus_ref, gate_ref, h_ref, lnw_ref, lnb_ref, g_ref, w_ref, bd_ref, out_ref, *, lp):
    tm = h_ref.shape[0]
    y = y_ref[...]
    mu = _head_sum(y, bd_ref) * (1.0 / RW_HEAD_DIM)
    dlt = y - mu
    var = _head_sum(dlt * dlt, bd_ref) * (1.0 / RW_HEAD_DIM)
    yn = dlt * lax.rsqrt(var + RW_LN_EPS) * lnw_ref[...] + lnb_ref[...]
    z = ((yn + bonus_ref[...]) * gate_ref[...]).astype(BF16)
    m = _dot(z, w_ref[...])
    upd = jnp.where(_row_valid(tm, lp), _rms(m, g_ref[...]), 0.0)
    out_ref[...] = h_ref[...] + upd


def _rw_out(y2, bonus, gate, h2, ln_w, ln_b, g_post, w_out, bd, tm, lp):
    n, d = h2.shape
    row = pl.BlockSpec((tm, d), lambda i: (i, 0))
    vec = _resident((1, d))
    return pl.pallas_call(
        functools.partial(_rw_out_body, lp=lp),
        grid=(n // tm,),
        in_specs=[row, row, row, row, vec, vec, vec, _resident(w_out.shape), _resident(bd.shape)],
        out_specs=row,
        out_shape=jax.ShapeDtypeStruct((n, d), F32),
        compiler_params=_params("parallel"),
        name="rw_out",
    )(y2, bonus, gate, h2, ln_w, ln_b, g_post, w_out, bd)


def _pad_cols(w, width):
    return jnp.pad(w, ((0, 0), (0, width - w.shape[1])))


def _pad_rows(w, height):
    return jnp.pad(w, ((0, height - w.shape[0]), (0, 0)))


def _rwkv7_mixer(h2, g_pre, g_post, mix, w_in, w0, w1, w2, a0, a1, a2, g1, g2, k_k, k_a, r_k, ln_w, ln_b, w_out,
                 b, lp, tm):
    d = h2.shape[1]
    idx = jnp.arange(d) // RW_HEAD_DIM
    bd = (idx[:, None] == idx[None, :]).astype(BF16)
    lora = lambda w_a, w_b: (_pad_cols(w_a, -(-w_a.shape[1] // LANES) * LANES).astype(BF16),
                             _pad_rows(w_b, -(-w_b.shape[0] // LANES) * LANES).astype(BF16))
    w1p, w2p = lora(w1, w2)
    a1p, a2p = lora(a1, a2)
    g1p, g2p = lora(g1, g2)
    outs = _rw_proj(h2, g_pre, mix, w_in.astype(BF16), w0[None], w1p, w2p, a0[None], a1p, a2p, g1p, g2p,
                    k_k[None], k_a[None], r_k.reshape(1, d), bd, RW_PROJ_ROWS, lp)
    r, lw, k, v, na, bb, gate, bonus = outs
    y = _rw_scan(*(t.reshape(b, lp, d) for t in (r, lw, k, v, na, bb)))
    return _rw_out(y.reshape(b * lp, d), bonus, gate, h2, ln_w[None], ln_b[None], g_post, w_out.astype(BF16), bd, tm, lp)


def kernel(x, meta_tokens, norm_gains, ffn_w_in, ffn_w_out, sb_w_in, sb_w_out, mla_w_in, mla_q_norm, mla_kv_norm, mla_w_uq, mla_w_ukv, mla_w_out, rw_mix, rw_w_in, rw_w0, rw_w1, rw_w2, rw_a0, rw_a1, rw_a2, rw_g1, rw_g2, rw_k_k, rw_k_a, rw_r_k, rw_ln_w, rw_ln_b, rw_w_out):
    b, seq, d = x.shape
    assert d == D_MODEL and seq % ATT_BLOCK == 0
    lp = PAD + seq
    tm = _row_tile(lp)
    meta = jnp.broadcast_to(meta_tokens.astype(x.dtype)[None], (b, N_META, d))
    h = jnp.concatenate([jnp.zeros((b, FIRST_VALID, d), x.dtype), meta, x], axis=1)
    h2 = h.reshape(b * lp, d)
    depth = norm_gains.shape[0]
    for i in range(depth):
        g = norm_gains[i][:, :, None, :]
        h2 = _ffn(h2, g[0, 0], g[0, 1], ffn_w_in[i, 0].astype(BF16), ffn_w_out[i, 0].astype(BF16), tm)
        kind, slot = i % 3, i // 3
        if kind == 0:
            h2 = _stick_breaking_mixer(h2, g[1, 0], g[1, 1], sb_w_in[slot], sb_w_out[slot], b, lp, tm)
        elif kind == 1:
            h2 = _mla_mixer(h2, g[1, 0], g[1, 1], mla_w_in[slot], mla_q_norm[slot], mla_kv_norm[slot],
                            mla_w_uq[slot], mla_w_ukv[slot], mla_w_out[slot], b, lp, tm)
        else:
            h2 = _rwkv7_mixer(h2, g[1, 0], g[1, 1], rw_mix[slot], rw_w_in[slot], rw_w0[slot], rw_w1[slot],
                              rw_w2[slot], rw_a0[slot], rw_a1[slot], rw_a2[slot], rw_g1[slot], rw_g2[slot],
                              rw_k_k[slot], rw_k_a[slot], rw_r_k[slot], rw_ln_w[slot], rw_ln_b[slot],
                              rw_w_out[slot], b, lp, tm)
        h2 = _ffn(h2, g[2, 0], g[2, 1], ffn_w_in[i, 1].astype(BF16), ffn_w_out[i, 1].astype(BF16), tm)
    return h2.reshape(b, lp, d)[:, PAD:]
```

```python
import functools
import math

import jax
import jax.numpy as jnp
from jax import lax
from jax.experimental import pallas as pl
from jax.experimental.pallas import tpu as pltpu

F32 = jnp.float32
BF16 = jnp.bfloat16

D_MODEL = 1024
N_META = 16
RMS_EPS = 1e-6
SB_HEADS = 8
SB_HEAD_DIM = D_MODEL // SB_HEADS
SB_SCALE = 1.0 / math.sqrt(SB_HEAD_DIM)
MLA_HEADS = 8
MLA_NOPE_DIM = 128
MLA_ROPE_DIM = 64
MLA_V_DIM = 128
MLA_Q_RANK = 3 * D_MODEL // 8
MLA_KV_RANK = D_MODEL // 4
MLA_SCALE = 1.0 / math.sqrt(MLA_NOPE_DIM + MLA_ROPE_DIM)
ROPE_THETA = 10000.0
RW_HEAD_DIM = 64
RW_LN_EPS = 64e-5
D_FF = ((8 * D_MODEL // 3 + 127) // 128) * 128

LANES = 128
MXU_DIM = 256
PAD = 256
FIRST_VALID = PAD - N_META
ATT_BLOCK = MXU_DIM
FFN_CHUNK = MXU_DIM
RW_CHUNK = RW_HEAD_DIM
RW_GROUP = MXU_DIM
RW_CHUNKS_PER_STEP = (10, 5, 4, 2, 1)
RW_PROJ_ROWS = 256
MASK_VALUE = -1e30
SB_UNDERFLOW_LOG = 110.0
VMEM_LIMIT = 56 * 1024 * 1024


def _row_tile(lp):
    for t in (640, 512, 256, 128):
        if lp % t == 0:
            return t
    raise ValueError(f"unsupported padded length {lp}")


def _params(*sem, flags=None):
    return pltpu.CompilerParams(dimension_semantics=sem, vmem_limit_bytes=VMEM_LIMIT, flags=flags)


def _resident(shape):
    nd = len(shape)
    return pl.BlockSpec(shape, lambda *_: (0,) * nd, pipeline_mode=pl.Buffered(1))


def _rms(x, g):
    return x * lax.rsqrt(jnp.mean(x * x, axis=-1, keepdims=True) + RMS_EPS) * g


def _dot(a, b):
    return jnp.dot(a, b, preferred_element_type=F32)


def _dot_nt(a, b):
    return lax.dot_general(a, b, (((1,), (1,)), ((), ())), preferred_element_type=F32)


def _dot_tn(a, b):
    return lax.dot_general(a, b, (((0,), (0,)), ((), ())), preferred_element_type=F32)


def _split(x):
    hi = x.astype(BF16)
    lo = (x - hi.astype(F32)).astype(BF16)
    return hi, lo


def _row_valid(tile_rows, lp):
    pos0 = (pl.program_id(0) * tile_rows) % lp
    pos = pos0 + lax.broadcasted_iota(jnp.int32, (tile_rows, 1), 0)
    return pos >= FIRST_VALID


def _ffn_body(x_ref, gpre_ref, gpost_ref, win_ref, wout_ref, o_ref):
    x = x_ref[...]
    xn = _rms(x, gpre_ref[...]).astype(BF16)
    acc = jnp.zeros(x.shape, F32)
    for c in range(D_FF // FFN_CHUNK):
        lo = c * FFN_CHUNK
        gate = _dot(xn, win_ref[:, lo:lo + FFN_CHUNK])
        up = _dot(xn, win_ref[:, D_FF + lo:D_FF + lo + FFN_CHUNK])
        act = (gate * jax.nn.sigmoid(gate) * up).astype(BF16)
        acc = acc + _dot(act, wout_ref[lo:lo + FFN_CHUNK, :])
    o_ref[...] = x + 0.5 * _rms(acc, gpost_ref[...])


def _ffn(h2, g_pre, g_post, w_in, w_out, tm):
    n, d = h2.shape
    row = pl.BlockSpec((tm, d), lambda i: (i, 0))
    return pl.pallas_call(
        _ffn_body,
        grid=(n // tm,),
        in_specs=[row, _resident((1, d)), _resident((1, d)), _resident(w_in.shape), _resident(w_out.shape)],
        out_specs=row,
        out_shape=jax.ShapeDtypeStruct((n, d), F32),
        compiler_params=_params("parallel"),
        name="ffn",
    )(h2, g_pre, g_post, w_in, w_out)


def _out_proj_body(o_ref, h_ref, g_ref, w_ref, out_ref, *, lp):
    tm = h_ref.shape[0]
    m = _dot(o_ref[...], w_ref[...])
    upd = jnp.where(_row_valid(tm, lp), _rms(m, g_ref[...]), 0.0)
    out_ref[...] = h_ref[...] + upd


def _out_proj(o2, h2, g_post, w_out, tm, lp):
    n, d = h2.shape
    return pl.pallas_call(
        functools.partial(_out_proj_body, lp=lp),
        grid=(n // tm,),
        in_specs=[pl.BlockSpec((tm, o2.shape[1]), lambda i: (i, 0)), pl.BlockSpec((tm, d), lambda i: (i, 0)),
                  _resident((1, d)), _resident(w_out.shape)],
        out_specs=pl.BlockSpec((tm, d), lambda i: (i, 0)),
        out_shape=jax.ShapeDtypeStruct((n, d), F32),
        compiler_params=_params("parallel"),
        name="out_proj",
    )(o2, h2, g_post, w_out)


def _sb_proj_body(x_ref, g_ref, w_ref, qkv_ref):
    u = _rms(x_ref[...], g_ref[...]).astype(BF16)
    qkv = _dot(u, w_ref[...])
    d = x_ref.shape[1]
    qkv_ref[:, :d] = (qkv[:, :d] * SB_SCALE).astype(BF16)
    qkv_ref[:, d:] = qkv[:, d:].astype(BF16)


def _sb_proj(h2, g_pre, w_in, tm):
    n, d = h2.shape
    return pl.pallas_call(
        _sb_proj_body,
        grid=(n // tm,),
        in_specs=[pl.BlockSpec((tm, d), lambda i: (i, 0)), _resident((1, d)), _resident(w_in.shape)],
        out_specs=pl.BlockSpec((tm, w_in.shape[1]), lambda i: (i, 0)),
        out_shape=jax.ShapeDtypeStruct((n, w_in.shape[1]), BF16),
        compiler_params=_params("parallel"),
        name="sb_proj",
    )(h2, g_pre, w_in)


def _walk_key_steps(top, front, back, state):
    def body(_, carried):
        step, fr, st = carried
        st = back(step, fr, st)
        return step - 1, front(step - 1, False), st

    _, fr, st = lax.fori_loop(0, top, body, (top, front(top, True), state))
    return back(0, fr, st)


def _reduce_rows(x, op, group=64):
    part = functools.reduce(op, [x[i:i + group] for i in range(0, x.shape[0], group)])
    return jnp.max(part, axis=0, keepdims=True) if op is jnp.maximum else jnp.sum(part, axis=0, keepdims=True)


def _lane_tile(x, reps):
    return x if reps == 1 else jnp.concatenate([x] * reps, axis=1)


def _sb_attn_body(q_ref, k_ref, v_ref, tri_ref, o_ref):
    tb = ATT_BLOCK
    n = q_ref.shape[1] // tb
    qi = pl.program_id(2)
    tri = tri_ref[...]
    sign = jnp.int32(-2 ** 31)
    last = FIRST_VALID // tb

    def step(c, kb, masked, carry, acc):
        k0 = pl.multiple_of(jnp.maximum(kb, 0) * tb, tb)
        k = k_ref[0, pl.ds(k0, tb), :]
        v = v_ref[0, pl.ds(k0, tb), :]
        z = _dot_nt(q_ref[0, c * tb:(c + 1) * tb, :], k)
        neg_abs = lax.bitcast_convert_type(lax.bitcast_convert_type(z, jnp.int32) | sign, F32)
        sp = jnp.maximum(z, 0.0) + jnp.log(1.0 + jnp.exp(neg_abs))
        if masked:
            row = (qi * n + c) * tb + lax.broadcasted_iota(jnp.int32, (tb, tb), 0)
            col = k0 + lax.broadcasted_iota(jnp.int32, (tb, tb), 1)
            valid = (col < row) & (col >= FIRST_VALID) & (kb >= last)
            sp = jnp.where(valid, sp, 0.0)
        within = _dot(sp.astype(BF16), tri)
        p = jnp.exp(z - within - _lane_tile(carry, tb // LANES))
        if masked:
            p = jnp.where(valid, p, 0.0)
        return carry + jnp.broadcast_to(within[:, 0:1], carry.shape), acc + _dot(p.astype(BF16), v)

    def round_(j, masked, state):
        return tuple(step(c, qi * n + c - j, masked, *state[c]) for c in range(n))

    def live(state):
        carry = functools.reduce(jnp.minimum, [s[0] for s in state])
        return jnp.min(carry) < SB_UNDERFLOW_LOG

    zero = (jnp.zeros((tb, LANES), F32), jnp.zeros((tb, SB_HEAD_DIM), F32))
    state = round_(0, True, (zero,) * n)
    j, state = lax.while_loop(lambda c: (qi * n - c[0] > last) & live(c[1]),
                              lambda c: (c[0] + 1, round_(c[0], False, c[1])), (1, state))
    j, state = lax.while_loop(lambda c: (qi * n + (n - 1) - c[0] >= last) & live(c[1]),
                              lambda c: (c[0] + 1, round_(c[0], True, c[1])), (j, state))
    for c in range(n):
        o_ref[0, c * tb:(c + 1) * tb, :] = state[c][1].astype(o_ref.dtype)


def _sb_chains(lp):
    return next(n for n in (5, 4, 3, 2, 1) if (lp // ATT_BLOCK) % n == 0)


def _sb_attn(qkv, tri):
    b, lp, _ = qkv.shape
    tb = ATT_BLOCK * _sb_chains(lp)
    hd = SB_HEAD_DIM
    return pl.pallas_call(
        _sb_attn_body,
        grid=(b, SB_HEADS, lp // tb),
        in_specs=[pl.BlockSpec((1, tb, hd), lambda bi, h, i: (bi, i, h)),
                  pl.BlockSpec((1, lp, hd), lambda bi, h, i: (bi, 0, SB_HEADS + h)),
                  pl.BlockSpec((1, lp, hd), lambda bi, h, i: (bi, 0, 2 * SB_HEADS + h)),
                  _resident(tri.shape)],
        out_specs=pl.BlockSpec((1, tb, hd), lambda bi, h, i: (bi, i, h)),
        out_shape=jax.ShapeDtypeStruct((b, lp, SB_HEADS * hd), BF16),
        compiler_params=_params("parallel", "parallel", "arbitrary"),
        name="sb_attn",
    )(qkv, qkv, qkv, tri)


def _stick_breaking_mixer(h2, g_pre, g_post, w_in, w_out, b, lp, tm):
    qkv = _sb_proj(h2, g_pre, w_in.astype(BF16), tm)
    idx = jnp.arange(ATT_BLOCK)
    tri = (idx[:, None] >= idx[None, :]).astype(BF16)
    o = _sb_attn(qkv.reshape(b, lp, -1), tri)
    return _out_proj(o.reshape(b * lp, -1), h2, g_post, w_out.astype(BF16), tm, lp)


MLA_QK_DIM = MXU_DIM
MLA_Q_SCALE = MLA_SCALE * math.log2(math.e)
MLA_KEY_BLOCKS = 4
MLA_Q_BLOCK = 1 * ATT_BLOCK

def _rope_tile(t, cos, sin):
    return t * cos + pltpu.roll(t, MLA_ROPE_DIM, 1) * sin


def _mla_proj_body(x_ref, g_ref, win_ref, qn_ref, kvn_ref, wuq_ref, wukv_ref, cos_ref, sin_ref, q_ref, k_ref, v_ref,
                   *, lp):
    u = _rms(x_ref[...], g_ref[...]).astype(BF16)
    c = _dot(u, win_ref[...])
    cq = _rms(c[:, :MLA_Q_RANK], qn_ref[...]).astype(BF16)
    ckv = _rms(c[:, MLA_Q_RANK:MLA_Q_RANK + MLA_KV_RANK], kvn_ref[...]).astype(BF16)
    cos = cos_ref[...]
    sin = sin_ref[...]
    tm = x_ref.shape[0]
    bias_lane = lax.broadcasted_iota(jnp.int32, (tm, LANES), 1) == MLA_ROPE_DIM
    q_bias = jnp.where(bias_lane, 1.0, 0.0)
    k_bias = jnp.where(bias_lane & jnp.logical_not(_row_valid(tm, lp)), MASK_VALUE, 0.0)
    k_rope = (_rope_tile(c[:, MLA_Q_RANK + MLA_KV_RANK:], cos, sin) + k_bias).astype(BF16)
    q = _dot(cq, wuq_ref[...])
    kv = _dot(ckv, wukv_ref[...])
    for h in range(MLA_HEADS):
        lo = h * MLA_QK_DIM
        q_ref[:, lo:lo + LANES] = (q[:, lo:lo + LANES] * MLA_Q_SCALE).astype(BF16)
        q_rope = _rope_tile(q[:, lo + LANES:lo + 2 * LANES], cos, sin)
        q_ref[:, lo + LANES:lo + 2 * LANES] = (q_rope * MLA_Q_SCALE + q_bias).astype(BF16)
        k_ref[:, lo:lo + LANES] = kv[:, h * LANES:(h + 1) * LANES].astype(BF16)
        k_ref[:, lo + LANES:lo + 2 * LANES] = k_rope
    v_ref[...] = kv[:, MLA_HEADS * MLA_NOPE_DIM:].astype(BF16)


def _mla_proj(h2, g_pre, w_in, q_norm, kv_norm, w_uq, w_ukv, cos, sin, tm, lp):
    n, d = h2.shape
    tiles_per_seq = lp // tm
    row = lambda w: pl.BlockSpec((tm, w), lambda i: (i, 0))
    table = pl.BlockSpec((tm, LANES), lambda i: (i % tiles_per_seq, 0))
    qk_w = MLA_HEADS * MLA_QK_DIM
    v_w = MLA_HEADS * MLA_V_DIM
    return pl.pallas_call(
        functools.partial(_mla_proj_body, lp=lp),
        grid=(n // tm,),
        in_specs=[row(d), _resident((1, d)), _resident(w_in.shape), _resident(q_norm.shape), _resident(kv_norm.shape),
                  _resident(w_uq.shape), _resident(w_ukv.shape), table, table],
        out_specs=[row(qk_w), row(qk_w), row(v_w)],
        out_shape=[jax.ShapeDtypeStruct((n, qk_w), BF16), jax.ShapeDtypeStruct((n, qk_w), BF16),
                   jax.ShapeDtypeStruct((n, v_w), BF16)],
        compiler_params=_params("parallel"),
        name="mla_proj",
    )(h2, g_pre, w_in, q_norm, kv_norm, w_uq, w_ukv, cos, sin)


def _mla_attn_body(q_ref, k_ref, vt_ref, o_ref, z_ref):
    tb = ATT_BLOCK
    per_step = MLA_KEY_BLOCKS
    tk = per_step * tb
    tq = q_ref.shape[1]
    lp = k_ref.shape[1]
    qi = pl.program_id(2)
    q = q_ref[0]

    def first_block(step):
        return jnp.minimum(step * per_step, lp // tb - per_step)

    def front(step, masked):
        kb0 = first_block(step)
        z = _dot_nt(k_ref[0, pl.ds(pl.multiple_of(kb0 * tb, tb), tk), :], q)
        if masked:
            key = kb0 * tb + lax.broadcasted_iota(jnp.int32, (tk, tq), 0)
            qry = qi * tq + lax.broadcasted_iota(jnp.int32, (tk, tq), 1)
            z = jnp.where((key <= qry) & (key >= jnp.maximum(step * tk, FIRST_VALID)), z, MASK_VALUE)
        z_ref[step & 1] = z
        return _reduce_rows(z, jnp.maximum)

    def back(step, z_max, state):
        z = z_ref[step & 1]
        m_prev, l_prev, acc = state
        kb0 = first_block(step)
        m_new = jnp.maximum(m_prev, z_max)
        p = jnp.exp2(z - m_new)
        alpha = jnp.exp2(m_prev - m_new)
        l_new = alpha * l_prev + _reduce_rows(p, jnp.add)
        vt = jnp.concatenate([vt_ref[0, 0, kb0 + i] for i in range(per_step)], axis=1)
        return m_new, l_new, alpha * acc + _dot(vt, p.astype(BF16))

    init = (jnp.full((1, tq), MASK_VALUE, F32), jnp.zeros((1, tq), F32), jnp.zeros((MLA_V_DIM, tq), F32))
    top = lax.shift_right_logical(qi * tq + (tq - 1), int(math.log2(tk)))
    _, l, acc = _walk_key_steps(top, front, back, init)
    o_ref[0] = (acc / l).T.astype(o_ref.dtype)


def _mla_attn(q, k, vt):
    b, lp, _ = q.shape
    tb = ATT_BLOCK
    tq = MLA_Q_BLOCK
    assert lp % tq == 0 and lp >= MLA_KEY_BLOCKS * tb and FIRST_VALID < MLA_KEY_BLOCKS * tb
    return pl.pallas_call(
        _mla_attn_body,
        grid=(b, MLA_HEADS, lp // tq),
        in_specs=[pl.BlockSpec((1, tq, MLA_QK_DIM), lambda bi, h, i: (bi, i, h)),
                  pl.BlockSpec((1, lp, MLA_QK_DIM), lambda bi, h, i: (bi, 0, h)),
                  pl.BlockSpec((1, 1, lp // tb, MLA_V_DIM, tb), lambda bi, h, i: (bi, h, 0, 0, 0))],
        out_specs=pl.BlockSpec((1, tq, MLA_V_DIM), lambda bi, h, i: (bi, i, h)),
        out_shape=jax.ShapeDtypeStruct((b, lp, MLA_HEADS * MLA_V_DIM), BF16),
        scratch_shapes=[pltpu.VMEM((2, MLA_KEY_BLOCKS * tb, tq), F32)],
        compiler_params=_params("parallel", "parallel", "arbitrary"),
        name="mla_attn",
    )(q, k, vt)


def _rotate_half_cols(w):
    half = MLA_ROPE_DIM // 2
    return jnp.concatenate([-w[..., half:], w[..., :half]], axis=-1)


def _mla_mixer(h2, g_pre, g_post, w_in, q_norm, kv_norm, w_uq, w_ukv, w_out, b, lp, tm):
    rank = MLA_Q_RANK + MLA_KV_RANK
    w_kr = w_in[:, rank:]
    w_in_x = jnp.concatenate([w_in[:, :rank], w_kr, _rotate_half_cols(w_kr)], axis=1).astype(BF16)
    wq = w_uq.reshape(MLA_Q_RANK, MLA_HEADS, MLA_NOPE_DIM + MLA_ROPE_DIM)
    wq_rope = wq[..., MLA_NOPE_DIM:]
    w_uq_x = jnp.concatenate([wq[..., :MLA_NOPE_DIM], wq_rope, _rotate_half_cols(wq_rope)], axis=-1)
    w_uq_x = w_uq_x.reshape(MLA_Q_RANK, MLA_HEADS * MLA_QK_DIM).astype(BF16)
    wkv = w_ukv.reshape(MLA_KV_RANK, MLA_HEADS, MLA_NOPE_DIM + MLA_V_DIM)
    w_ukv_x = jnp.concatenate([wkv[..., :MLA_NOPE_DIM].reshape(MLA_KV_RANK, -1),
                               wkv[..., MLA_NOPE_DIM:].reshape(MLA_KV_RANK, -1)], axis=1).astype(BF16)
    half = MLA_ROPE_DIM // 2
    inv_freq = ROPE_THETA ** (-jnp.arange(half, dtype=F32) / half)
    pos = jnp.arange(lp) - FIRST_VALID
    ang = pos.astype(F32)[:, None] * inv_freq[None, :]
    zeros = jnp.zeros((lp, LANES - MLA_ROPE_DIM), F32)
    cos = jnp.concatenate([jnp.cos(ang), jnp.cos(ang), zeros], axis=1)
    sin = jnp.concatenate([jnp.sin(ang), jnp.sin(ang), zeros], axis=1)
    q, k, v = _mla_proj(h2, g_pre, w_in_x, q_norm[None], kv_norm[None], w_uq_x, w_ukv_x, cos, sin, tm, lp)
    vt = jnp.transpose(v.reshape(b, lp // ATT_BLOCK, ATT_BLOCK, MLA_HEADS, MLA_V_DIM), (0, 3, 1, 4, 2))
    o = _mla_attn(q.reshape(b, lp, -1), k.reshape(b, lp, -1), vt)
    return _out_proj(o.reshape(b * lp, -1), h2, g_post, w_out.astype(BF16), tm, lp)


def _head_sum(x, bd_ref):
    return _dot(x.astype(BF16), bd_ref[...])


def _rw_proj_body(x_ref, xp_ref, g_ref, mix_ref, wrkv_ref, w0_ref, w1_ref, w2_ref, a0_ref, a1_ref, a2_ref,
                  g1_ref, g2_ref, kk_ref, ka_ref, rk_ref, bd_ref,
                  r_ref, lw_ref, k_ref, v_ref, na_ref, b_ref, gate_ref, bonus_ref, *, lp):
    tm = x_ref.shape[0]
    g = g_ref[...]
    u = _rms(x_ref[...], g)
    first = (pl.program_id(0) * tm) % lp == 0
    u_prev = jnp.where(first, 0.0, _rms(xp_ref[...], g)[7:8, :])
    shifted = jnp.where(lax.broadcasted_iota(jnp.int32, (tm, 1), 0) == 0, u_prev, pltpu.roll(u, 1, 0))
    xx = shifted - u
    mix = mix_ref[...]
    xs = lambda n: (u + xx * mix[n:n + 1, :]).astype(BF16)
    r = _dot(xs(0), wrkv_ref[0])
    k = _dot(xs(1), wrkv_ref[1])
    v = _dot(xs(2), wrkv_ref[2])
    wl = w0_ref[...] + _dot(jnp.tanh(_dot(xs(3), w1_ref[...])).astype(BF16), w2_ref[...])
    w_log = -(jnp.maximum(-wl, 0.0) + jnp.log(1.0 + jnp.exp(-jnp.abs(wl)))) - 0.5
    lw_ref[...] = -jnp.exp(w_log)
    a = jax.nn.sigmoid(a0_ref[...] + _dot(_dot(xs(4), a1_ref[...]).astype(BF16), a2_ref[...]))
    gate = _dot(jax.nn.sigmoid(_dot(xs(5), g1_ref[...])).astype(BF16), g2_ref[...])
    gate_ref[...] = gate.astype(gate_ref.dtype)
    kk = k * kk_ref[...]
    kk = kk / jnp.maximum(jnp.sqrt(_head_sum(kk * kk, bd_ref)), 1e-12)
    k2 = k * (1.0 + (a - 1.0) * ka_ref[...])
    r_ref[...] = r.astype(r_ref.dtype)
    k_ref[...] = k2.astype(k_ref.dtype)
    v_ref[...] = v.astype(v_ref.dtype)
    na_ref[...] = (-kk).astype(na_ref.dtype)
    b_ref[...] = (kk * a).astype(b_ref.dtype)
    bonus_ref[...] = _head_sum(r * k2 * rk_ref[...], bd_ref) * v


def _rw_proj(h2, g_pre, mix, w_rkv, w0, w1, w2, a0, a1, a2, g1, g2, k_k, k_a, r_k, bd, tm, lp):
    n, d = h2.shape
    row = pl.BlockSpec((tm, d), lambda i: (i, 0))
    prev = pl.BlockSpec((8, d), lambda i: (jnp.maximum(i * (tm // 8) - 1, 0), 0))
    vec = _resident((1, d))
    outs = [jax.ShapeDtypeStruct((n, d), dt) for dt in (BF16, F32, BF16, BF16, BF16, BF16, BF16, F32)]
    return pl.pallas_call(
        functools.partial(_rw_proj_body, lp=lp),
        grid=(n // tm,),
        in_specs=[row, prev, vec, _resident(mix.shape), _resident(w_rkv.shape), vec, _resident(w1.shape),
                  _resident(w2.shape), vec, _resident(a1.shape), _resident(a2.shape), _resident(g1.shape),
                  _resident(g2.shape), vec, vec, vec, _resident(bd.shape)],
        out_specs=[row] * 8,
        out_shape=outs,
        compiler_params=_params("parallel"),
        name="rw_proj",
    )(h2, h2, g_pre, mix, w_rkv, w0, w1, w2, a0, a1, a2, g1, g2, k_k, k_a, r_k, bd)


def _rw_scan_body(r_ref, lw_ref, k_ref, v_ref, a_ref, b_ref, y_ref, s_ref):
    c = RW_CHUNK
    gw = RW_GROUP
    heads = gw // c

    @pl.when(pl.program_id(2) == 0)
    def _():
        s_ref[...] = jnp.zeros_like(s_ref)

    ri = lax.broadcasted_iota(jnp.int32, (gw, gw), 0)
    ci = lax.broadcasted_iota(jnp.int32, (gw, gw), 1)
    shift = int(math.log2(c))
    block_mask = (ri >> shift) == (ci >> shift)
    t_idx = lax.broadcasted_iota(jnp.int32, (c, gw), 0)
    s_idx = lax.broadcasted_iota(jnp.int32, (c, gw), 1) & (c - 1)
    strict = s_idx < t_idx
    incl = s_idx <= t_idx
    eye = (s_idx == t_idx).astype(F32)
    tri = (lax.broadcasted_iota(jnp.int32, (c, c), 1) <= lax.broadcasted_iota(jnp.int32, (c, c), 0)).astype(BF16)

    def bd(x):
        return jnp.where(block_mask, jnp.concatenate([x.astype(BF16)] * heads, axis=0), 0.0)

    def lower(x, mask):
        return jnp.where(mask, x, 0.0).astype(BF16)

    n_chunks = r_ref.shape[1] // c
    pre = []
    for j in range(n_chunks):
        rows = slice(j * c, (j + 1) * c)
        r, lw, k, v, a, b = (ref[0, rows, :] for ref in (r_ref, lw_ref, k_ref, v_ref, a_ref, b_ref))
        cum = _cumsum_rows(lw, tri)
        total = cum[c - 1:c, :]
        lhs = jnp.concatenate([a * jnp.exp(cum - lw), r * jnp.exp(cum)], axis=0).astype(BF16)
        g_inv = jnp.exp(-cum)
        a_b = _dot_nt(lhs, bd(b * g_inv))
        a_k = _dot_nt(lhs, bd(k * g_inv))
        g_out = jnp.exp(total - cum)
        rhs_out = jnp.concatenate([b * g_out, k * g_out], axis=0).astype(BF16)
        pre.append(dict(lhs=lhs, v=v, total=total, rhs_out=rhs_out, n=jnp.where(strict, a_b[:c], 0.0),
                        a_ak=lower(a_k[:c], strict), a_rb=lower(a_b[c:], incl), a_rk=lower(a_k[c:], incl)))
    t_mats = [eye + p["n"] for p in pre]
    pws = [p["n"] for p in pre]
    for _ in range(int(math.log2(c)) - 1):
        pws = [_dot(pw.astype(BF16), bd(pw)) for pw in pws]
        t_mats = [t + _dot(t.astype(BF16), bd(pw)) for t, pw in zip(t_mats, pws)]
    affine = []
    for j in range(n_chunks):
        p = pre[j]
        t16 = t_mats[j].astype(BF16)
        bd_v = bd(p["v"])
        w1 = _dot(t16, bd(p["lhs"][:c]))
        u0 = _dot(t16, bd(_dot(p["a_ak"], bd_v)))
        r2 = (p["lhs"][c:].astype(F32) + _dot(p["a_rb"], bd(w1))).astype(BF16)
        y0 = _dot(p["a_rb"], bd(u0)) + _dot(p["a_rk"], bd_v)
        p_mat = jnp.where(block_mask, _dot_tn(w1.astype(BF16), p["rhs_out"][:c]), 0.0).astype(BF16)
        n0 = jnp.where(block_mask, _dot_tn(jnp.concatenate([u0, p["v"]], axis=0).astype(BF16), p["rhs_out"]), 0.0)
        affine.append((r2, y0, p_mat, n0, jnp.exp(p["total"])))
    s_t = s_ref[...]
    for j in range(n_chunks):
        r2, y0, p_mat, n0, decay = affine[j]
        s16 = s_t.astype(BF16)
        y_ref[0, j * c:(j + 1) * c, :] = _dot_nt(r2, s16) + y0
        s_t = s_t * decay + _dot(s16, p_mat) + n0
    s_ref[...] = s_t


def _cumsum_rows(x, tri):
    hi, lo = _split(x)
    return _dot(tri, hi) + _dot(tri, lo)


def _rw_scan(r, lw, k, v, a, b):
    bsz, lp, d = r.shape
    rows = RW_CHUNK * next(n for n in RW_CHUNKS_PER_STEP if lp % (RW_CHUNK * n) == 0)
    spec = pl.BlockSpec((1, rows, RW_GROUP), lambda bi, g, i: (bi, i, g))
    return pl.pallas_call(
        _rw_scan_body,
        grid=(bsz, d // RW_GROUP, lp // rows),
        in_specs=[spec] * 6,
        out_specs=spec,
        out_shape=jax.ShapeDtypeStruct((bsz, lp, d), F32),
        scratch_shapes=[pltpu.VMEM((RW_GROUP, RW_GROUP), F32)],
        compiler_params=_params("parallel", "parallel", "arbitrary"),
        name="rw_scan",
    )(r, lw, k, v, a, b)


def _rw_out_body(y_ref, bonus_ref, gate_ref, h_ref, lnw_ref, lnb_ref, g_ref, w_ref, bd_ref, out_ref, *, lp):
    tm = h_ref.shape[0]
    y = y_ref[...]
    mu = _head_sum(y, bd_ref) * (1.0 / RW_HEAD_DIM)
    dlt = y - mu
    var = _head_sum(dlt * dlt, bd_ref) * (1.0 / RW_HEAD_DIM)
    yn = dlt * lax.rsqrt(var + RW_LN_EPS) * lnw_ref[...] + lnb_ref[...]
    z = ((yn + bonus_ref[...]) * gate_ref[...]).astype(BF16)
    m = _dot(z, w_ref[...])
    upd = jnp.where(_row_valid(tm, lp), _rms(m, g_ref[...]), 0.0)
    out_ref[...] = h_ref[...] + upd


def _rw_out(y2, bonus, gate, h2, ln_w, ln_b, g_post, w_out, bd, tm, lp):
    n, d = h2.shape
    row = pl.BlockSpec((tm, d), lambda i: (i, 0))
    vec = _resident((1, d))
    return pl.pallas_call(
        functools.partial(_rw_out_body, lp=lp),
        grid=(n // tm,),
        in_specs=[row, row, row, row, vec, vec, vec, _resident(w_out.shape), _resident(bd.shape)],
        out_specs=row,
        out_shape=jax.ShapeDtypeStruct((n, d), F32),
        compiler_params=_params("parallel"),
        name="rw_out",
    )(y2, bonus, gate, h2, ln_w, ln_b, g_post, w_out, bd)


def _pad_cols(w, width):
    return jnp.pad(w, ((0, 0), (0, width - w.shape[1])))


def _pad_rows(w, height):
    return jnp.pad(w, ((0, height - w.shape[0]), (0, 0)))


def _rwkv7_mixer(h2, g_pre, g_post, mix, w_in, w0, w1, w2, a0, a1, a2, g1, g2, k_k, k_a, r_k, ln_w, ln_b, w_out,
                 b, lp, tm):
    d = h2.shape[1]
    idx = jnp.arange(d) // RW_HEAD_DIM
    bd = (idx[:, None] == idx[None, :]).astype(BF16)
    lora = lambda w_a, w_b: (_pad_cols(w_a, -(-w_a.shape[1] // LANES) * LANES).astype(BF16),
                             _pad_rows(w_b, -(-w_b.shape[0] // LANES) * LANES).astype(BF16))
    w1p, w2p = lora(w1, w2)
    a1p, a2p = lora(a1, a2)
    g1p, g2p = lora(g1, g2)
    outs = _rw_proj(h2, g_pre, mix, w_in.astype(BF16), w0[None], w1p, w2p, a0[None], a1p, a2p, g1p, g2p,
                    k_k[None], k_a[None], r_k.reshape(1, d), bd, RW_PROJ_ROWS, lp)
    r, lw, k, v, na, bb, gate, bonus = outs
    y = _rw_scan(*(t.reshape(b, lp, d) for t in (r, lw, k, v, na, bb)))
    return _rw_out(y.reshape(b * lp, d), bonus, gate, h2, ln_w[None], ln_b[None], g_post, w_out.astype(BF16), bd, tm, lp)


def kernel(x, meta_tokens, norm_gains, ffn_w_in, ffn_w_out, sb_w_in, sb_w_out, mla_w_in, mla_q_norm, mla_kv_norm, mla_w_uq, mla_w_ukv, mla_w_out, rw_mix, rw_w_in, rw_w0, rw_w1, rw_w2, rw_a0, rw_a1, rw_a2, rw_g1, rw_g2, rw_k_k, rw_k_a, rw_r_k, rw_ln_w, rw_ln_b, rw_w_out):
    b, seq, d = x.shape
    assert d == D_MODEL and seq % ATT_BLOCK == 0
    lp = PAD + seq
    tm = _row_tile(lp)
    meta = jnp.broadcast_to(meta_tokens.astype(x.dtype)[None], (b, N_META, d))
    h = jnp.concatenate([jnp.zeros((b, FIRST_VALID, d), x.dtype), meta, x], axis=1)
    h2 = h.reshape(b * lp, d)
    depth = norm_gains.shape[0]
    for i in range(depth):
        g = norm_gains[i][:, :, None, :]
        h2 = _ffn(h2, g[0, 0], g[0, 1], ffn_w_in[i, 0].astype(BF16), ffn_w_out[i, 0].astype(BF16), tm)
        kind, slot = i % 3, i // 3
        if kind == 0:
            h2 = _stick_breaking_mixer(h2, g[1, 0], g[1, 1], sb_w_in[slot], sb_w_out[slot], b, lp, tm)
        elif kind == 1:
            h2 = _mla_mixer(h2, g[1, 0], g[1, 1], mla_w_in[slot], mla_q_norm[slot], mla_kv_norm[slot],
                            mla_w_uq[slot], mla_w_ukv[slot], mla_w_out[slot], b, lp, tm)
        else:
            h2 = _rwkv7_mixer(h2, g[1, 0], g[1, 1], rw_mix[slot], rw_w_in[slot], rw_w0[slot], rw_w1[slot],
                              rw_w2[slot], rw_a0[slot], rw_a1[slot], rw_a2[slot], rw_g1[slot], rw_g2[slot],
                              rw_k_k[slot], rw_k_a[slot], rw_r_k[slot], rw_ln_w[slot], rw_ln_b[slot],
                              rw_w_out[slot], b, lp, tm)
        h2 = _ffn(h2, g[2, 0], g[2, 1], ffn_w_in[i, 1].astype(BF16), ffn_w_out[i, 1].astype(BF16), tm)
    return h2.reshape(b, lp, d)[:, PAD:]
```

```python
import functools
import math

import jax
import jax.numpy as jnp
from jax import lax
from jax.experimental import pallas as pl
from jax.experimental.pallas import tpu as pltpu

F32 = jnp.float32
BF16 = jnp.bfloat16

D_MODEL = 1024
N_META = 16
RMS_EPS = 1e-6
SB_HEADS = 8
SB_HEAD_DIM = D_MODEL // SB_HEADS
SB_SCALE = 1.0 / math.sqrt(SB_HEAD_DIM)
MLA_HEADS = 8
MLA_NOPE_DIM = 128
MLA_ROPE_DIM = 64
MLA_V_DIM = 128
MLA_Q_RANK = 3 * D_MODEL // 8
MLA_KV_RANK = D_MODEL // 4
MLA_SCALE = 1.0 / math.sqrt(MLA_NOPE_DIM + MLA_ROPE_DIM)
ROPE_THETA = 10000.0
RW_HEAD_DIM = 64
RW_LN_EPS = 64e-5
D_FF = ((8 * D_MODEL // 3 + 127) // 128) * 128

LANES = 128
MXU_DIM = 256
PAD = 256
FIRST_VALID = PAD - N_META
ATT_BLOCK = MXU_DIM
FFN_CHUNK = MXU_DIM
RW_CHUNK = RW_HEAD_DIM
RW_GROUP = MXU_DIM
RW_CHUNKS_PER_STEP = (10, 5, 4, 2, 1)
RW_PROJ_ROWS = 256
MASK_VALUE = -1e30
SB_UNDERFLOW_LOG = 110.0
VMEM_LIMIT = 56 * 1024 * 1024


def _row_tile(lp):
    for t in (640, 512, 256, 128):
        if lp % t == 0:
            return t
    raise ValueError(f"unsupported padded length {lp}")


def _params(*sem, flags=None):
    return pltpu.CompilerParams(dimension_semantics=sem, vmem_limit_bytes=VMEM_LIMIT, flags=flags)


def _resident(shape):
    nd = len(shape)
    return pl.BlockSpec(shape, lambda *_: (0,) * nd, pipeline_mode=pl.Buffered(1))


def _rms(x, g):
    return x * lax.rsqrt(jnp.mean(x * x, axis=-1, keepdims=True) + RMS_EPS) * g


def _dot(a, b):
    return jnp.dot(a, b, preferred_element_type=F32)


def _dot_nt(a, b):
    return lax.dot_general(a, b, (((1,), (1,)), ((), ())), preferred_element_type=F32)


def _dot_tn(a, b):
    return lax.dot_general(a, b, (((0,), (0,)), ((), ())), preferred_element_type=F32)


def _split(x):
    hi = x.astype(BF16)
    lo = (x - hi.astype(F32)).astype(BF16)
    return hi, lo


def _row_valid(tile_rows, lp):
    pos0 = (pl.program_id(0) * tile_rows) % lp
    pos = pos0 + lax.broadcasted_iota(jnp.int32, (tile_rows, 1), 0)
    return pos >= FIRST_VALID


def _ffn_body(x_ref, gpre_ref, gpost_ref, win_ref, wout_ref, o_ref):
    x = x_ref[...]
    xn = _rms(x, gpre_ref[...]).astype(BF16)
    acc = jnp.zeros(x.shape, F32)
    for c in range(D_FF // FFN_CHUNK):
        lo = c * FFN_CHUNK
        gate = _dot(xn, win_ref[:, lo:lo + FFN_CHUNK])
        up = _dot(xn, win_ref[:, D_FF + lo:D_FF + lo + FFN_CHUNK])
        act = (gate * jax.nn.sigmoid(gate) * up).astype(BF16)
        acc = acc + _dot(act, wout_ref[lo:lo + FFN_CHUNK, :])
    o_ref[...] = x + 0.5 * _rms(acc, gpost_ref[...])


def _ffn(h2, g_pre, g_post, w_in, w_out, tm):
    n, d = h2.shape
    row = pl.BlockSpec((tm, d), lambda i: (i, 0))
    return pl.pallas_call(
        _ffn_body,
        grid=(n // tm,),
        in_specs=[row, _resident((1, d)), _resident((1, d)), _resident(w_in.shape), _resident(w_out.shape)],
        out_specs=row,
        out_shape=jax.ShapeDtypeStruct((n, d), F32),
        compiler_params=_params("parallel"),
        name="ffn",
    )(h2, g_pre, g_post, w_in, w_out)


def _out_proj_body(o_ref, h_ref, g_ref, w_ref, out_ref, *, lp):
    tm = h_ref.shape[0]
    m = _dot(o_ref[...], w_ref[...])
    upd = jnp.where(_row_valid(tm, lp), _rms(m, g_ref[...]), 0.0)
    out_ref[...] = h_ref[...] + upd


def _out_proj(o2, h2, g_post, w_out, tm, lp):
    n, d = h2.shape
    return pl.pallas_call(
        functools.partial(_out_proj_body, lp=lp),
        grid=(n // tm,),
        in_specs=[pl.BlockSpec((tm, o2.shape[1]), lambda i: (i, 0)), pl.BlockSpec((tm, d), lambda i: (i, 0)),
                  _resident((1, d)), _resident(w_out.shape)],
        out_specs=pl.BlockSpec((tm, d), lambda i: (i, 0)),
        out_shape=jax.ShapeDtypeStruct((n, d), F32),
        compiler_params=_params("parallel"),
        name="out_proj",
    )(o2, h2, g_post, w_out)


def _sb_proj_body(x_ref, g_ref, w_ref, qkv_ref):
    u = _rms(x_ref[...], g_ref[...]).astype(BF16)
    qkv = _dot(u, w_ref[...])
    d = x_ref.shape[1]
    qkv_ref[:, :d] = (qkv[:, :d] * SB_SCALE).astype(BF16)
    qkv_ref[:, d:] = qkv[:, d:].astype(BF16)


def _sb_proj(h2, g_pre, w_in, tm):
    n, d = h2.shape
    return pl.pallas_call(
        _sb_proj_body,
        grid=(n // tm,),
        in_specs=[pl.BlockSpec((tm, d), lambda i: (i, 0)), _resident((1, d)), _resident(w_in.shape)],
        out_specs=pl.BlockSpec((tm, w_in.shape[1]), lambda i: (i, 0)),
        out_shape=jax.ShapeDtypeStruct((n, w_in.shape[1]), BF16),
        compiler_params=_params("parallel"),
        name="sb_proj",
    )(h2, g_pre, w_in)


def _walk_key_steps(top, lowest_masked, front, back, state):
    def trip(masked):
        def body(_, carried):
            step, fr, st = carried
            st = back(step, fr, st)
            return step - 1, front(step - 1, masked), st
        return body

    carried = (top, front(top, True), state)
    carried = lax.fori_loop(0, top - lowest_masked, trip(True), carried)
    _, fr, st = lax.fori_loop(0, lowest_masked, trip(False), carried)
    return back(0, fr, st)


def _reduce_rows(x, op, group=64):
    part = functools.reduce(op, [x[i:i + group] for i in range(0, x.shape[0], group)])
    return jnp.max(part, axis=0, keepdims=True) if op is jnp.maximum else jnp.sum(part, axis=0, keepdims=True)


def _lane_tile(x, reps):
    return x if reps == 1 else jnp.concatenate([x] * reps, axis=1)


def _sb_attn_body(q_ref, k_ref, v_ref, tri_ref, o_ref):
    tb = ATT_BLOCK
    n = q_ref.shape[1] // tb
    qi = pl.program_id(2)
    tri = tri_ref[...]
    sign = jnp.int32(-2 ** 31)
    last = FIRST_VALID // tb

    def step(c, kb, masked, carry, acc):
        k0 = pl.multiple_of(jnp.maximum(kb, 0) * tb, tb)
        k = k_ref[0, pl.ds(k0, tb), :]
        v = v_ref[0, pl.ds(k0, tb), :]
        z = _dot_nt(q_ref[0, c * tb:(c + 1) * tb, :], k)
        neg_abs = lax.bitcast_convert_type(lax.bitcast_convert_type(z, jnp.int32) | sign, F32)
        sp = jnp.maximum(z, 0.0) + jnp.log(1.0 + jnp.exp(neg_abs))
        if masked:
            row = (qi * n + c) * tb + lax.broadcasted_iota(jnp.int32, (tb, tb), 0)
            col = k0 + lax.broadcasted_iota(jnp.int32, (tb, tb), 1)
            valid = (col < row) & (col >= FIRST_VALID) & (kb >= last)
            sp = jnp.where(valid, sp, 0.0)
        within = _dot(sp.astype(BF16), tri)
        p = jnp.exp(z - within - _lane_tile(carry, tb // LANES))
        if masked:
            p = jnp.where(valid, p, 0.0)
        return carry + jnp.broadcast_to(within[:, 0:1], carry.shape), acc + _dot(p.astype(BF16), v)

    def round_(j, masked, state):
        return tuple(step(c, qi * n + c - j, masked, *state[c]) for c in range(n))

    def live(state):
        carry = functools.reduce(jnp.minimum, [s[0] for s in state])
        return jnp.min(carry) < SB_UNDERFLOW_LOG

    zero = (jnp.zeros((tb, LANES), F32), jnp.zeros((tb, SB_HEAD_DIM), F32))
    state = round_(0, True, (zero,) * n)
    j, state = lax.while_loop(lambda c: (qi * n - c[0] > last) & live(c[1]),
                              lambda c: (c[0] + 1, round_(c[0], False, c[1])), (1, state))
    j, state = lax.while_loop(lambda c: (qi * n + (n - 1) - c[0] >= last) & live(c[1]),
                              lambda c: (c[0] + 1, round_(c[0], True, c[1])), (j, state))
    for c in range(n):
        o_ref[0, c * tb:(c + 1) * tb, :] = state[c][1].astype(o_ref.dtype)


def _sb_chains(lp):
    return next(n for n in (5, 4, 3, 2, 1) if (lp // ATT_BLOCK) % n == 0)


def _sb_attn(qkv, tri):
    b, lp, _ = qkv.shape
    tb = ATT_BLOCK * _sb_chains(lp)
    hd = SB_HEAD_DIM
    return pl.pallas_call(
        _sb_attn_body,
        grid=(b, SB_HEADS, lp // tb),
        in_specs=[pl.BlockSpec((1, tb, hd), lambda bi, h, i: (bi, i, h)),
                  pl.BlockSpec((1, lp, hd), lambda bi, h, i: (bi, 0, SB_HEADS + h)),
                  pl.BlockSpec((1, lp, hd), lambda bi, h, i: (bi, 0, 2 * SB_HEADS + h)),
                  _resident(tri.shape)],
        out_specs=pl.BlockSpec((1, tb, hd), lambda bi, h, i: (bi, i, h)),
        out_shape=jax.ShapeDtypeStruct((b, lp, SB_HEADS * hd), BF16),
        compiler_params=_params("parallel", "parallel", "arbitrary"),
        name="sb_attn",
    )(qkv, qkv, qkv, tri)


def _stick_breaking_mixer(h2, g_pre, g_post, w_in, w_out, b, lp, tm):
    qkv = _sb_proj(h2, g_pre, w_in.astype(BF16), tm)
    idx = jnp.arange(ATT_BLOCK)
    tri = (idx[:, None] >= idx[None, :]).astype(BF16)
    o = _sb_attn(qkv.reshape(b, lp, -1), tri)
    return _out_proj(o.reshape(b * lp, -1), h2, g_post, w_out.astype(BF16), tm, lp)


MLA_QK_DIM = MXU_DIM
MLA_Q_SCALE = MLA_SCALE * math.log2(math.e)
MLA_KEY_BLOCKS = 4

def _rope_tile(t, cos, sin):
    return t * cos + pltpu.roll(t, MLA_ROPE_DIM, 1) * sin


def _mla_proj_body(x_ref, g_ref, win_ref, qn_ref, kvn_ref, wuq_ref, wukv_ref, cos_ref, sin_ref, q_ref, k_ref, v_ref,
                   *, lp):
    u = _rms(x_ref[...], g_ref[...]).astype(BF16)
    c = _dot(u, win_ref[...])
    cq = _rms(c[:, :MLA_Q_RANK], qn_ref[...]).astype(BF16)
    ckv = _rms(c[:, MLA_Q_RANK:MLA_Q_RANK + MLA_KV_RANK], kvn_ref[...]).astype(BF16)
    cos = cos_ref[...]
    sin = sin_ref[...]
    tm = x_ref.shape[0]
    bias_lane = lax.broadcasted_iota(jnp.int32, (tm, LANES), 1) == MLA_ROPE_DIM
    q_bias = jnp.where(bias_lane, 1.0, 0.0)
    k_bias = jnp.where(bias_lane & jnp.logical_not(_row_valid(tm, lp)), MASK_VALUE, 0.0)
    k_rope = (_rope_tile(c[:, MLA_Q_RANK + MLA_KV_RANK:], cos, sin) + k_bias).astype(BF16)
    q = _dot(cq, wuq_ref[...])
    kv = _dot(ckv, wukv_ref[...])
    for h in range(MLA_HEADS):
        lo = h * MLA_QK_DIM
        q_ref[:, lo:lo + LANES] = (q[:, lo:lo + LANES] * MLA_Q_SCALE).astype(BF16)
        q_rope = _rope_tile(q[:, lo + LANES:lo + 2 * LANES], cos, sin)
        q_ref[:, lo + LANES:lo + 2 * LANES] = (q_rope * MLA_Q_SCALE + q_bias).astype(BF16)
        k_ref[:, lo:lo + LANES] = kv[:, h * LANES:(h + 1) * LANES].astype(BF16)
        k_ref[:, lo + LANES:lo + 2 * LANES] = k_rope
    v_ref[...] = kv[:, MLA_HEADS * MLA_NOPE_DIM:].astype(BF16)


def _mla_proj(h2, g_pre, w_in, q_norm, kv_norm, w_uq, w_ukv, cos, sin, tm, lp):
    n, d = h2.shape
    tiles_per_seq = lp // tm
    row = lambda w: pl.BlockSpec((tm, w), lambda i: (i, 0))
    table = pl.BlockSpec((tm, LANES), lambda i: (i % tiles_per_seq, 0))
    qk_w = MLA_HEADS * MLA_QK_DIM
    v_w = MLA_HEADS * MLA_V_DIM
    return pl.pallas_call(
        functools.partial(_mla_proj_body, lp=lp),
        grid=(n // tm,),
        in_specs=[row(d), _resident((1, d)), _resident(w_in.shape), _resident(q_norm.shape), _resident(kv_norm.shape),
                  _resident(w_uq.shape), _resident(w_ukv.shape), table, table],
        out_specs=[row(qk_w), row(qk_w), row(v_w)],
        out_shape=[jax.ShapeDtypeStruct((n, qk_w), BF16), jax.ShapeDtypeStruct((n, qk_w), BF16),
                   jax.ShapeDtypeStruct((n, v_w), BF16)],
        compiler_params=_params("parallel"),
        name="mla_proj",
    )(h2, g_pre, w_in, q_norm, kv_norm, w_uq, w_ukv, cos, sin)


def _mla_attn_body(q_ref, k_ref, vt_ref, o_ref, z_ref):
    tb = ATT_BLOCK
    per_step = MLA_KEY_BLOCKS
    tk = per_step * tb
    tq = q_ref.shape[1]
    lp = k_ref.shape[1]
    qi = pl.program_id(2)
    q = q_ref[0]

    def first_block(step):
        return jnp.minimum(step * per_step, lp // tb - per_step)

    def front(step, masked):
        kb0 = first_block(step)
        z = _dot_nt(k_ref[0, pl.ds(pl.multiple_of(kb0 * tb, tb), tk), :], q)
        if masked:
            key = kb0 * tb + lax.broadcasted_iota(jnp.int32, (tk, tq), 0)
            qry = qi * tq + lax.broadcasted_iota(jnp.int32, (tk, tq), 1)
            z = jnp.where((key <= qry) & (key >= jnp.maximum(step * tk, FIRST_VALID)), z, MASK_VALUE)
        z_ref[step & 1] = z
        return _reduce_rows(z, jnp.maximum)

    def back(step, z_max, state):
        z = z_ref[step & 1]
        m_prev, l_prev, acc = state
        kb0 = first_block(step)
        m_new = jnp.maximum(m_prev, z_max)
        p = jnp.exp2(z - m_new)
        alpha = jnp.exp2(m_prev - m_new)
        l_new = alpha * l_prev + _reduce_rows(p, jnp.add)
        vt = jnp.concatenate([vt_ref[0, 0, kb0 + i] for i in range(per_step)], axis=1)
        return m_new, l_new, alpha * acc + _dot(vt, p.astype(BF16))

    init = (jnp.full((1, tq), MASK_VALUE, F32), jnp.zeros((1, tq), F32), jnp.zeros((MLA_V_DIM, tq), F32))
    top = lax.shift_right_logical(qi * tq + (tq - 1), int(math.log2(tk)))
    lowest = lax.shift_right_logical(qi * tq, int(math.log2(tk)))
    _, l, acc = _walk_key_steps(top, lowest, front, back, init)
    o_ref[0] = (acc / l).T.astype(o_ref.dtype)


def _mla_attn(q, k, vt):
    b, lp, _ = q.shape
    tb = ATT_BLOCK
    tq = tb * _sb_chains(lp)
    assert lp >= MLA_KEY_BLOCKS * tb and FIRST_VALID < MLA_KEY_BLOCKS * tb
    return pl.pallas_call(
        _mla_attn_body,
        grid=(b, MLA_HEADS, lp // tq),
        in_specs=[pl.BlockSpec((1, tq, MLA_QK_DIM), lambda bi, h, i: (bi, i, h)),
                  pl.BlockSpec((1, lp, MLA_QK_DIM), lambda bi, h, i: (bi, 0, h)),
                  pl.BlockSpec((1, 1, lp // tb, MLA_V_DIM, tb), lambda bi, h, i: (bi, h, 0, 0, 0))],
        out_specs=pl.BlockSpec((1, tq, MLA_V_DIM), lambda bi, h, i: (bi, i, h)),
        out_shape=jax.ShapeDtypeStruct((b, lp, MLA_HEADS * MLA_V_DIM), BF16),
        scratch_shapes=[pltpu.VMEM((2, MLA_KEY_BLOCKS * tb, tq), F32)],
        compiler_params=_params("parallel", "parallel", "arbitrary"),
        name="mla_attn",
    )(q, k, vt)


def _rotate_half_cols(w):
    half = MLA_ROPE_DIM // 2
    return jnp.concatenate([-w[..., half:], w[..., :half]], axis=-1)


def _mla_mixer(h2, g_pre, g_post, w_in, q_norm, kv_norm, w_uq, w_ukv, w_out, b, lp, tm):
    rank = MLA_Q_RANK + MLA_KV_RANK
    w_kr = w_in[:, rank:]
    w_in_x = jnp.concatenate([w_in[:, :rank], w_kr, _rotate_half_cols(w_kr)], axis=1).astype(BF16)
    wq = w_uq.reshape(MLA_Q_RANK, MLA_HEADS, MLA_NOPE_DIM + MLA_ROPE_DIM)
    wq_rope = wq[..., MLA_NOPE_DIM:]
    w_uq_x = jnp.concatenate([wq[..., :MLA_NOPE_DIM], wq_rope, _rotate_half_cols(wq_rope)], axis=-1)
    w_uq_x = w_uq_x.reshape(MLA_Q_RANK, MLA_HEADS * MLA_QK_DIM).astype(BF16)
    wkv = w_ukv.reshape(MLA_KV_RANK, MLA_HEADS, MLA_NOPE_DIM + MLA_V_DIM)
    w_ukv_x = jnp.concatenate([wkv[..., :MLA_NOPE_DIM].reshape(MLA_KV_RANK, -1),
                               wkv[..., MLA_NOPE_DIM:].reshape(MLA_KV_RANK, -1)], axis=1).astype(BF16)
    half = MLA_ROPE_DIM // 2
    inv_freq = ROPE_THETA ** (-jnp.arange(half, dtype=F32) / half)
    pos = jnp.arange(lp) - FIRST_VALID
    ang = pos.astype(F32)[:, None] * inv_freq[None, :]
    zeros = jnp.zeros((lp, LANES - MLA_ROPE_DIM), F32)
    cos = jnp.concatenate([jnp.cos(ang), jnp.cos(ang), zeros], axis=1)
    sin = jnp.concatenate([jnp.sin(ang), jnp.sin(ang), zeros], axis=1)
    q, k, v = _mla_proj(h2, g_pre, w_in_x, q_norm[None], kv_norm[None], w_uq_x, w_ukv_x, cos, sin, tm, lp)
    vt = jnp.transpose(v.reshape(b, lp // ATT_BLOCK, ATT_BLOCK, MLA_HEADS, MLA_V_DIM), (0, 3, 1, 4, 2))
    o = _mla_attn(q.reshape(b, lp, -1), k.reshape(b, lp, -1), vt)
    return _out_proj(o.reshape(b * lp, -1), h2, g_post, w_out.astype(BF16), tm, lp)


def _head_sum(x, bd_ref):
    return _dot(x.astype(BF16), bd_ref[...])


def _rw_proj_body(x_ref, xp_ref, g_ref, mix_ref, wrkv_ref, w0_ref, w1_ref, w2_ref, a0_ref, a1_ref, a2_ref,
                  g1_ref, g2_ref, kk_ref, ka_ref, rk_ref, bd_ref,
                  r_ref, lw_ref, k_ref, v_ref, na_ref, b_ref, gate_ref, bonus_ref, *, lp):
    tm = x_ref.shape[0]
    g = g_ref[...]
    u = _rms(x_ref[...], g)
    first = (pl.program_id(0) * tm) % lp == 0
    u_prev = jnp.where(first, 0.0, _rms(xp_ref[...], g)[7:8, :])
    shifted = jnp.where(lax.broadcasted_iota(jnp.int32, (tm, 1), 0) == 0, u_prev, pltpu.roll(u, 1, 0))
    xx = shifted - u
    mix = mix_ref[...]
    xs = lambda n: (u + xx * mix[n:n + 1, :]).astype(BF16)
    r = _dot(xs(0), wrkv_ref[0])
    k = _dot(xs(1), wrkv_ref[1])
    v = _dot(xs(2), wrkv_ref[2])
    wl = w0_ref[...] + _dot(jnp.tanh(_dot(xs(3), w1_ref[...])).astype(BF16), w2_ref[...])
    w_log = -(jnp.maximum(-wl, 0.0) + jnp.log(1.0 + jnp.exp(-jnp.abs(wl)))) - 0.5
    lw_ref[...] = -jnp.exp(w_log)
    a = jax.nn.sigmoid(a0_ref[...] + _dot(_dot(xs(4), a1_ref[...]).astype(BF16), a2_ref[...]))
    gate = _dot(jax.nn.sigmoid(_dot(xs(5), g1_ref[...])).astype(BF16), g2_ref[...])
    gate_ref[...] = gate.astype(gate_ref.dtype)
    kk = k * kk_ref[...]
    kk = kk / jnp.maximum(jnp.sqrt(_head_sum(kk * kk, bd_ref)), 1e-12)
    k2 = k * (1.0 + (a - 1.0) * ka_ref[...])
    r_ref[...] = r.astype(r_ref.dtype)
    k_ref[...] = k2.astype(k_ref.dtype)
    v_ref[...] = v.astype(v_ref.dtype)
    na_ref[...] = (-kk).astype(na_ref.dtype)
    b_ref[...] = (kk * a).astype(b_ref.dtype)
    bonus_ref[...] = _head_sum(r * k2 * rk_ref[...], bd_ref) * v


def _rw_proj(h2, g_pre, mix, w_rkv, w0, w1, w2, a0, a1, a2, g1, g2, k_k, k_a, r_k, bd, tm, lp):
    n, d = h2.shape
    row = pl.BlockSpec((tm, d), lambda i: (i, 0))
    prev = pl.BlockSpec((8, d), lambda i: (jnp.maximum(i * (tm // 8) - 1, 0), 0))
    vec = _resident((1, d))
    outs = [jax.ShapeDtypeStruct((n, d), dt) for dt in (BF16, F32, BF16, BF16, BF16, BF16, BF16, F32)]
    return pl.pallas_call(
        functools.partial(_rw_proj_body, lp=lp),
        grid=(n // tm,),
        in_specs=[row, prev, vec, _resident(mix.shape), _resident(w_rkv.shape), vec, _resident(w1.shape),
                  _resident(w2.shape), vec, _resident(a1.shape), _resident(a2.shape), _resident(g1.shape),
                  _resident(g2.shape), vec, vec, vec, _resident(bd.shape)],
        out_specs=[row] * 8,
        out_shape=outs,
        compiler_params=_params("parallel"),
        name="rw_proj",
    )(h2, h2, g_pre, mix, w_rkv, w0, w1, w2, a0, a1, a2, g1, g2, k_k, k_a, r_k, bd)


def _rw_scan_body(r_ref, lw_ref, k_ref, v_ref, a_ref, b_ref, y_ref, s_ref):
    c = RW_CHUNK
    gw = RW_GROUP
    heads = gw // c

    @pl.when(pl.program_id(2) == 0)
    def _():
        s_ref[...] = jnp.zeros_like(s_ref)

    ri = lax.broadcasted_iota(jnp.int32, (gw, gw), 0)
    ci = lax.broadcasted_iota(jnp.int32, (gw, gw), 1)
    shift = int(math.log2(c))
    block_mask = (ri >> shift) == (ci >> shift)
    t_idx = lax.broadcasted_iota(jnp.int32, (c, gw), 0)
    s_idx = lax.broadcasted_iota(jnp.int32, (c, gw), 1) & (c - 1)
    strict = s_idx < t_idx
    incl = s_idx <= t_idx
    eye = (s_idx == t_idx).astype(F32)
    tri = (lax.broadcasted_iota(jnp.int32, (c, c), 1) <= lax.broadcasted_iota(jnp.int32, (c, c), 0)).astype(BF16)

    def bd(x):
        return jnp.where(block_mask, jnp.concatenate([x.astype(BF16)] * heads, axis=0), 0.0)

    def lower(x, mask):
        return jnp.where(mask, x, 0.0).astype(BF16)

    n_chunks = r_ref.shape[1] // c
    pre = []
    for j in range(n_chunks):
        rows = slice(j * c, (j + 1) * c)
        r, lw, k, v, a, b = (ref[0, rows, :] for ref in (r_ref, lw_ref, k_ref, v_ref, a_ref, b_ref))
        cum = _cumsum_rows(lw, tri)
        total = cum[c - 1:c, :]
        lhs = jnp.concatenate([a * jnp.exp(cum - lw), r * jnp.exp(cum)], axis=0).astype(BF16)
        g_inv = jnp.exp(-cum)
        a_b = _dot_nt(lhs, bd(b * g_inv))
        a_k = _dot_nt(lhs, bd(k * g_inv))
        g_out = jnp.exp(total - cum)
        rhs_out = jnp.concatenate([b * g_out, k * g_out], axis=0).astype(BF16)
        pre.append(dict(lhs=lhs, v=v, total=total, rhs_out=rhs_out, n=jnp.where(strict, a_b[:c], 0.0),
                        a_ak=lower(a_k[:c], strict), a_rb=lower(a_b[c:], incl), a_rk=lower(a_k[c:], incl)))
    t_mats = [eye + p["n"] for p in pre]
    pws = [p["n"] for p in pre]
    for _ in range(int(math.log2(c)) - 1):
        pws = [_dot(pw.astype(BF16), bd(pw)) for pw in pws]
        t_mats = [t + _dot(t.astype(BF16), bd(pw)) for t, pw in zip(t_mats, pws)]
    affine = []
    for j in range(n_chunks):
        p = pre[j]
        t16 = t_mats[j].astype(BF16)
        bd_v = bd(p["v"])
        w1 = _dot(t16, bd(p["lhs"][:c]))
        u0 = _dot(t16, bd(_dot(p["a_ak"], bd_v)))
        r2 = (p["lhs"][c:].astype(F32) + _dot(p["a_rb"], bd(w1))).astype(BF16)
        y0 = _dot(p["a_rb"], bd(u0)) + _dot(p["a_rk"], bd_v)
        p_mat = jnp.where(block_mask, _dot_tn(w1.astype(BF16), p["rhs_out"][:c]), 0.0).astype(BF16)
        n0 = jnp.where(block_mask, _dot_tn(jnp.concatenate([u0, p["v"]], axis=0).astype(BF16), p["rhs_out"]), 0.0)
        affine.append((r2, y0, p_mat, n0, jnp.exp(p["total"])))
    s_t = s_ref[...]
    for j in range(n_chunks):
        r2, y0, p_mat, n0, decay = affine[j]
        s16 = s_t.astype(BF16)
        y_ref[0, j * c:(j + 1) * c, :] = _dot_nt(r2, s16) + y0
        s_t = s_t * decay + _dot(s16, p_mat) + n0
    s_ref[...] = s_t


def _cumsum_rows(x, tri):
    hi, lo = _split(x)
    return _dot(tri, hi) + _dot(tri, lo)


def _rw_scan(r, lw, k, v, a, b):
    bsz, lp, d = r.shape
    rows = RW_CHUNK * next(n for n in RW_CHUNKS_PER_STEP if lp % (RW_CHUNK * n) == 0)
    spec = pl.BlockSpec((1, rows, RW_GROUP), lambda bi, g, i: (bi, i, g))
    return pl.pallas_call(
        _rw_scan_body,
        grid=(bsz, d // RW_GROUP, lp // rows),
        in_specs=[spec] * 6,
        out_specs=spec,
        out_shape=jax.ShapeDtypeStruct((bsz, lp, d), F32),
        scratch_shapes=[pltpu.VMEM((RW_GROUP, RW_GROUP), F32)],
        compiler_params=_params("parallel", "parallel", "arbitrary"),
        name="rw_scan",
    )(r, lw, k, v, a, b)


def _rw_out_body(y_ref, bonus_ref, gate_ref, h_ref, lnw_ref, lnb_ref, g_ref, w_ref, bd_ref, out_ref, *, lp):
    tm = h_ref.shape[0]
    y = y_ref[...]
    mu = _head_sum(y, bd_ref) * (1.0 / RW_HEAD_DIM)
    dlt = y - mu
    var = _head_sum(dlt * dlt, bd_ref) * (1.0 / RW_HEAD_DIM)
    yn = dlt * lax.rsqrt(var + RW_LN_EPS) * lnw_ref[...] + lnb_ref[...]
    z = ((yn + bonus_ref[...]) * gate_ref[...]).astype(BF16)
    m = _dot(z, w_ref[...])
    upd = jnp.where(_row_valid(tm, lp), _rms(m, g_ref[...]), 0.0)
    out_ref[...] = h_ref[...] + upd


def _rw_out(y2, bonus, gate, h2, ln_w, ln_b, g_post, w_out, bd, tm, lp):
    n, d = h2.shape
    row = pl.BlockSpec((tm, d), lambda i: (i, 0))
    vec = _resident((1, d))
    return pl.pallas_call(
        functools.partial(_rw_out_body, lp=lp),
        grid=(n // tm,),
        in_specs=[row, row, row, row, vec, vec, vec, _resident(w_out.shape), _resident(bd.shape)],
        out_specs=row,
        out_shape=jax.ShapeDtypeStruct((n, d), F32),
        compiler_params=_params("parallel"),
        name="rw_out",
    )(y2, bonus, gate, h2, ln_w, ln_b, g_post, w_out, bd)


def _pad_cols(w, width):
    return jnp.pad(w, ((0, 0), (0, width - w.shape[1])))


def _pad_rows(w, height):
    return jnp.pad(w, ((0, height - w.shape[0]), (0, 0)))


def _rwkv7_mixer(h2, g_pre, g_post, mix, w_in, w0, w1, w2, a0, a1, a2, g1, g2, k_k, k_a, r_k, ln_w, ln_b, w_out,
                 b, lp, tm):
    d = h2.shape[1]
    idx = jnp.arange(d) // RW_HEAD_DIM
    bd = (idx[:, None] == idx[None, :]).astype(BF16)
    lora = lambda w_a, w_b: (_pad_cols(w_a, -(-w_a.shape[1] // LANES) * LANES).astype(BF16),
                             _pad_rows(w_b, -(-w_b.shape[0] // LANES) * LANES).astype(BF16))
    w1p, w2p = lora(w1, w2)
    a1p, a2p = lora(a1, a2)
    g1p, g2p = lora(g1, g2)
    outs = _rw_proj(h2, g_pre, mix, w_in.astype(BF16), w0[None], w1p, w2p, a0[None], a1p, a2p, g1p, g2p,
                    k_k[None], k_a[None], r_k.reshape(1, d), bd, RW_PROJ_ROWS, lp)
    r, lw, k, v, na, bb, gate, bonus = outs
    y = _rw_scan(*(t.reshape(b, lp, d) for t in (r, lw, k, v, na, bb)))
    return _rw_out(y.reshape(b * lp, d), bonus, gate, h2, ln_w[None], ln_b[None], g_post, w_out.astype(BF16), bd, tm, lp)


def kernel(x, meta_tokens, norm_gains, ffn_w_in, ffn_w_out, sb_w_in, sb_w_out, mla_w_in, mla_q_norm, mla_kv_norm, mla_w_uq, mla_w_ukv, mla_w_out, rw_mix, rw_w_in, rw_w0, rw_w1, rw_w2, rw_a0, rw_a1, rw_a2, rw_g1, rw_g2, rw_k_k, rw_k_a, rw_r_k, rw_ln_w, rw_ln_b, rw_w_out):
    b, seq, d = x.shape
    assert d == D_MODEL and seq % ATT_BLOCK == 0
    lp = PAD + seq
    tm = _row_tile(lp)
    meta = jnp.broadcast_to(meta_tokens.astype(x.dtype)[None], (b, N_META, d))
    h = jnp.concatenate([jnp.zeros((b, FIRST_VALID, d), x.dtype), meta, x], axis=1)
    h2 = h.reshape(b * lp, d)
    depth = norm_gains.shape[0]
    for i in range(depth):
        g = norm_gains[i][:, :, None, :]
        h2 = _ffn(h2, g[0, 0], g[0, 1], ffn_w_in[i, 0].astype(BF16), ffn_w_out[i, 0].astype(BF16), tm)
        kind, slot = i % 3, i // 3
        if kind == 0:
            h2 = _stick_breaking_mixer(h2, g[1, 0], g[1, 1], sb_w_in[slot], sb_w_out[slot], b, lp, tm)
        elif kind == 1:
            h2 = _mla_mixer(h2, g[1, 0], g[1, 1], mla_w_in[slot], mla_q_norm[slot], mla_kv_norm[slot],
                            mla_w_uq[slot], mla_w_ukv[slot], mla_w_out[slot], b, lp, tm)
        else:
            h2 = _rwkv7_mixer(h2, g[1, 0], g[1, 1], rw_mix[slot], rw_w_in[slot], rw_w0[slot], rw_w1[slot],
                              rw_w2[slot], rw_a0[slot], rw_a1[slot], rw_a2[slot], rw_g1[slot], rw_g2[slot],
                              rw_k_k[slot], rw_k_a[slot], rw_r_k[slot], rw_ln_w[slot], rw_ln_b[slot],
                              rw_w_out[slot], b, lp, tm)
        h2 = _ffn(h2, g[2, 0], g[2, 1], ffn_w_in[i, 1].astype(BF16), ffn_w_out[i, 1].astype(BF16), tm)
    return h2.reshape(b, lp, d)[:, PAD:]
```

```python
import functools
import math

import jax
import jax.numpy as jnp
from jax import lax
from jax.experimental import pallas as pl
from jax.experimental.pallas import tpu as pltpu

F32 = jnp.float32
BF16 = jnp.bfloat16

D_MODEL = 1024
N_META = 16
RMS_EPS = 1e-6
SB_HEADS = 8
SB_HEAD_DIM = D_MODEL // SB_HEADS
SB_SCALE = 1.0 / math.sqrt(SB_HEAD_DIM)
MLA_HEADS = 8
MLA_NOPE_DIM = 128
MLA_ROPE_DIM = 64
MLA_V_DIM = 128
MLA_Q_RANK = 3 * D_MODEL // 8
MLA_KV_RANK = D_MODEL // 4
MLA_SCALE = 1.0 / math.sqrt(MLA_NOPE_DIM + MLA_ROPE_DIM)
ROPE_THETA = 10000.0
RW_HEAD_DIM = 64
RW_LN_EPS = 64e-5
D_FF = ((8 * D_MODEL // 3 + 127) // 128) * 128

LANES = 128
MXU_DIM = 256
PAD = 256
FIRST_VALID = PAD - N_META
ATT_BLOCK = MXU_DIM
FFN_CHUNK = MXU_DIM
RW_CHUNK = RW_HEAD_DIM
RW_GROUP = MXU_DIM
RW_CHUNKS_PER_STEP = (10, 5, 4, 2, 1)
RW_PROJ_ROWS = 256
MASK_VALUE = -1e30
SB_UNDERFLOW_LOG = 110.0
SB_CHAINS = (13, 5, 4, 3, 2, 1)
VMEM_LIMIT = 56 * 1024 * 1024


def _row_tile(lp):
    for t in (640, 512, 256, 128):
        if lp % t == 0:
            return t
    raise ValueError(f"unsupported padded length {lp}")


def _params(*sem, flags=None):
    return pltpu.CompilerParams(dimension_semantics=sem, vmem_limit_bytes=VMEM_LIMIT, flags=flags)


def _resident(shape):
    nd = len(shape)
    return pl.BlockSpec(shape, lambda *_: (0,) * nd, pipeline_mode=pl.Buffered(1))


def _rms(x, g):
    return x * lax.rsqrt(jnp.mean(x * x, axis=-1, keepdims=True) + RMS_EPS) * g


def _dot(a, b):
    return jnp.dot(a, b, preferred_element_type=F32)


def _dot_nt(a, b):
    return lax.dot_general(a, b, (((1,), (1,)), ((), ())), preferred_element_type=F32)


def _dot_tn(a, b):
    return lax.dot_general(a, b, (((0,), (0,)), ((), ())), preferred_element_type=F32)


def _split(x):
    hi = x.astype(BF16)
    lo = (x - hi.astype(F32)).astype(BF16)
    return hi, lo


def _row_valid(tile_rows, lp):
    pos0 = (pl.program_id(0) * tile_rows) % lp
    pos = pos0 + lax.broadcasted_iota(jnp.int32, (tile_rows, 1), 0)
    return pos >= FIRST_VALID


def _ffn_body(x_ref, gpre_ref, gpost_ref, win_ref, wout_ref, o_ref):
    x = x_ref[...]
    xn = _rms(x, gpre_ref[...]).astype(BF16)
    acc = jnp.zeros(x.shape, F32)
    for c in range(D_FF // FFN_CHUNK):
        lo = c * FFN_CHUNK
        gate = _dot(xn, win_ref[:, lo:lo + FFN_CHUNK])
        up = _dot(xn, win_ref[:, D_FF + lo:D_FF + lo + FFN_CHUNK])
        act = (gate * jax.nn.sigmoid(gate) * up).astype(BF16)
        acc = acc + _dot(act, wout_ref[lo:lo + FFN_CHUNK, :])
    o_ref[...] = x + 0.5 * _rms(acc, gpost_ref[...])


def _ffn(h2, g_pre, g_post, w_in, w_out, tm):
    n, d = h2.shape
    row = pl.BlockSpec((tm, d), lambda i: (i, 0))
    return pl.pallas_call(
        _ffn_body,
        grid=(n // tm,),
        in_specs=[row, _resident((1, d)), _resident((1, d)), _resident(w_in.shape), _resident(w_out.shape)],
        out_specs=row,
        out_shape=jax.ShapeDtypeStruct((n, d), F32),
        compiler_params=_params("parallel"),
        name="ffn",
    )(h2, g_pre, g_post, w_in, w_out)


def _out_proj_body(o_ref, h_ref, g_ref, w_ref, out_ref, *, lp):
    tm = h_ref.shape[0]
    m = _dot(o_ref[...], w_ref[...])
    upd = jnp.where(_row_valid(tm, lp), _rms(m, g_ref[...]), 0.0)
    out_ref[...] = h_ref[...] + upd


def _out_proj(o2, h2, g_post, w_out, tm, lp):
    n, d = h2.shape
    return pl.pallas_call(
        functools.partial(_out_proj_body, lp=lp),
        grid=(n // tm,),
        in_specs=[pl.BlockSpec((tm, o2.shape[1]), lambda i: (i, 0)), pl.BlockSpec((tm, d), lambda i: (i, 0)),
                  _resident((1, d)), _resident(w_out.shape)],
        out_specs=pl.BlockSpec((tm, d), lambda i: (i, 0)),
        out_shape=jax.ShapeDtypeStruct((n, d), F32),
        compiler_params=_params("parallel"),
        name="out_proj",
    )(o2, h2, g_post, w_out)


def _sb_proj_body(x_ref, g_ref, w_ref, qkv_ref):
    u = _rms(x_ref[...], g_ref[...]).astype(BF16)
    qkv = _dot(u, w_ref[...])
    d = x_ref.shape[1]
    qkv_ref[:, :d] = (qkv[:, :d] * SB_SCALE).astype(BF16)
    qkv_ref[:, d:] = qkv[:, d:].astype(BF16)


def _sb_proj(h2, g_pre, w_in, tm):
    n, d = h2.shape
    return pl.pallas_call(
        _sb_proj_body,
        grid=(n // tm,),
        in_specs=[pl.BlockSpec((tm, d), lambda i: (i, 0)), _resident((1, d)), _resident(w_in.shape)],
        out_specs=pl.BlockSpec((tm, w_in.shape[1]), lambda i: (i, 0)),
        out_shape=jax.ShapeDtypeStruct((n, w_in.shape[1]), BF16),
        compiler_params=_params("parallel"),
        name="sb_proj",
    )(h2, g_pre, w_in)


def _walk_key_steps(top, lowest_masked, front, back, state):
    def trip(masked):
        def body(_, carried):
            step, fr, st = carried
            st = back(step, fr, st)
            return step - 1, front(step - 1, masked), st
        return body

    carried = (top, front(top, True), state)
    carried = lax.fori_loop(0, top - lowest_masked, trip(True), carried)
    _, fr, st = lax.fori_loop(0, lowest_masked, trip(False), carried)
    return back(0, fr, st)


def _reduce_rows(x, op, group=64):
    part = functools.reduce(op, [x[i:i + group] for i in range(0, x.shape[0], group)])
    return jnp.max(part, axis=0, keepdims=True) if op is jnp.maximum else jnp.sum(part, axis=0, keepdims=True)


def _lane_tile(x, reps):
    return x if reps == 1 else jnp.concatenate([x] * reps, axis=1)


def _sb_attn_body(q_ref, k_ref, v_ref, tri_ref, o_ref):
    tb = ATT_BLOCK
    n = q_ref.shape[1] // tb
    qi = pl.program_id(2)
    tri = tri_ref[...]
    sign = jnp.int32(-2 ** 31)
    last = FIRST_VALID // tb

    def step(c, kb, masked, carry, acc):
        k0 = pl.multiple_of(jnp.maximum(kb, 0) * tb, tb)
        k = k_ref[0, pl.ds(k0, tb), :]
        v = v_ref[0, pl.ds(k0, tb), :]
        z = _dot_nt(q_ref[0, c * tb:(c + 1) * tb, :], k)
        neg_abs = lax.bitcast_convert_type(lax.bitcast_convert_type(z, jnp.int32) | sign, F32)
        sp = jnp.maximum(z, 0.0) + jnp.log(1.0 + jnp.exp(neg_abs))
        if masked:
            row = (qi * n + c) * tb + lax.broadcasted_iota(jnp.int32, (tb, tb), 0)
            col = k0 + lax.broadcasted_iota(jnp.int32, (tb, tb), 1)
            valid = (col < row) & (col >= FIRST_VALID) & (kb >= last)
            sp = jnp.where(valid, sp, 0.0)
        within = _dot(sp.astype(BF16), tri)
        p = jnp.exp(z - within - _lane_tile(carry, tb // LANES))
        if masked:
            p = jnp.where(valid, p, 0.0)
        return carry + jnp.broadcast_to(within[:, 0:1], carry.shape), acc + _dot(p.astype(BF16), v)

    def round_(j, masked, state):
        return tuple(step(c, qi * n + c - j, masked, *state[c]) for c in range(n))

    def live(state):
        carry = functools.reduce(jnp.minimum, [s[0] for s in state])
        return jnp.min(carry) < SB_UNDERFLOW_LOG

    zero = (jnp.zeros((tb, LANES), F32), jnp.zeros((tb, SB_HEAD_DIM), F32))
    state = round_(0, True, (zero,) * n)
    j, state = lax.while_loop(lambda c: (qi * n - c[0] > last) & live(c[1]),
                              lambda c: (c[0] + 1, round_(c[0], False, c[1])), (1, state))
    j, state = lax.while_loop(lambda c: (qi * n + (n - 1) - c[0] >= last) & live(c[1]),
                              lambda c: (c[0] + 1, round_(c[0], True, c[1])), (j, state))
    for c in range(n):
        o_ref[0, c * tb:(c + 1) * tb, :] = state[c][1].astype(o_ref.dtype)


def _blocks_per_step(lp, candidates):
    return next(n for n in candidates if (lp // ATT_BLOCK) % n == 0)


def _sb_attn(qkv, tri):
    b, lp, _ = qkv.shape
    tb = ATT_BLOCK * _blocks_per_step(lp, SB_CHAINS)
    hd = SB_HEAD_DIM
    return pl.pallas_call(
        _sb_attn_body,
        grid=(b, SB_HEADS, lp // tb),
        in_specs=[pl.BlockSpec((1, tb, hd), lambda bi, h, i: (bi, i, h)),
                  pl.BlockSpec((1, lp, hd), lambda bi, h, i: (bi, 0, SB_HEADS + h)),
                  pl.BlockSpec((1, lp, hd), lambda bi, h, i: (bi, 0, 2 * SB_HEADS + h)),
                  _resident(tri.shape)],
        out_specs=pl.BlockSpec((1, tb, hd), lambda bi, h, i: (bi, i, h)),
        out_shape=jax.ShapeDtypeStruct((b, lp, SB_HEADS * hd), BF16),
        compiler_params=_params("parallel", "parallel", "arbitrary"),
        name="sb_attn",
    )(qkv, qkv, qkv, tri)


def _stick_breaking_mixer(h2, g_pre, g_post, w_in, w_out, b, lp, tm):
    qkv = _sb_proj(h2, g_pre, w_in.astype(BF16), tm)
    idx = jnp.arange(ATT_BLOCK)
    tri = (idx[:, None] >= idx[None, :]).astype(BF16)
    o = _sb_attn(qkv.reshape(b, lp, -1), tri)
    return _out_proj(o.reshape(b * lp, -1), h2, g_post, w_out.astype(BF16), tm, lp)


MLA_QK_DIM = MXU_DIM
MLA_Q_BLOCKS = (5, 4, 3, 2, 1)
MLA_Q_SCALE = MLA_SCALE * math.log2(math.e)
MLA_KEY_BLOCKS = 4

def _rope_tile(t, cos, sin):
    return t * cos + pltpu.roll(t, MLA_ROPE_DIM, 1) * sin


def _mla_proj_body(x_ref, g_ref, win_ref, qn_ref, kvn_ref, wuq_ref, wukv_ref, cos_ref, sin_ref, q_ref, k_ref, v_ref,
                   *, lp):
    u = _rms(x_ref[...], g_ref[...]).astype(BF16)
    c = _dot(u, win_ref[...])
    cq = _rms(c[:, :MLA_Q_RANK], qn_ref[...]).astype(BF16)
    ckv = _rms(c[:, MLA_Q_RANK:MLA_Q_RANK + MLA_KV_RANK], kvn_ref[...]).astype(BF16)
    cos = cos_ref[...]
    sin = sin_ref[...]
    tm = x_ref.shape[0]
    bias_lane = lax.broadcasted_iota(jnp.int32, (tm, LANES), 1) == MLA_ROPE_DIM
    q_bias = jnp.where(bias_lane, 1.0, 0.0)
    k_bias = jnp.where(bias_lane & jnp.logical_not(_row_valid(tm, lp)), MASK_VALUE, 0.0)
    k_rope = (_rope_tile(c[:, MLA_Q_RANK + MLA_KV_RANK:], cos, sin) + k_bias).astype(BF16)
    q = _dot(cq, wuq_ref[...])
    kv = _dot(ckv, wukv_ref[...])
    for h in range(MLA_HEADS):
        lo = h * MLA_QK_DIM
        q_ref[:, lo:lo + LANES] = (q[:, lo:lo + LANES] * MLA_Q_SCALE).astype(BF16)
        q_rope = _rope_tile(q[:, lo + LANES:lo + 2 * LANES], cos, sin)
        q_ref[:, lo + LANES:lo + 2 * LANES] = (q_rope * MLA_Q_SCALE + q_bias).astype(BF16)
        k_ref[:, lo:lo + LANES] = kv[:, h * LANES:(h + 1) * LANES].astype(BF16)
        k_ref[:, lo + LANES:lo + 2 * LANES] = k_rope
    v_ref[...] = kv[:, MLA_HEADS * MLA_NOPE_DIM:].astype(BF16)


def _mla_proj(h2, g_pre, w_in, q_norm, kv_norm, w_uq, w_ukv, cos, sin, tm, lp):
    n, d = h2.shape
    tiles_per_seq = lp // tm
    row = lambda w: pl.BlockSpec((tm, w), lambda i: (i, 0))
    table = pl.BlockSpec((tm, LANES), lambda i: (i % tiles_per_seq, 0))
    qk_w = MLA_HEADS * MLA_QK_DIM
    v_w = MLA_HEADS * MLA_V_DIM
    return pl.pallas_call(
        functools.partial(_mla_proj_body, lp=lp),
        grid=(n // tm,),
        in_specs=[row(d), _resident((1, d)), _resident(w_in.shape), _resident(q_norm.shape), _resident(kv_norm.shape),
                  _resident(w_uq.shape), _resident(w_ukv.shape), table, table],
        out_specs=[row(qk_w), row(qk_w), row(v_w)],
        out_shape=[jax.ShapeDtypeStruct((n, qk_w), BF16), jax.ShapeDtypeStruct((n, qk_w), BF16),
                   jax.ShapeDtypeStruct((n, v_w), BF16)],
        compiler_params=_params("parallel"),
        name="mla_proj",
    )(h2, g_pre, w_in, q_norm, kv_norm, w_uq, w_ukv, cos, sin)


def _mla_attn_body(q_ref, k_ref, vt_ref, o_ref, z_ref):
    tb = ATT_BLOCK
    per_step = MLA_KEY_BLOCKS
    tk = per_step * tb
    tq = q_ref.shape[1]
    lp = k_ref.shape[1]
    qi = pl.program_id(2)
    q = q_ref[0]

    def first_block(step):
        return jnp.minimum(step * per_step, lp // tb - per_step)

    def front(step, masked):
        kb0 = first_block(step)
        z = _dot_nt(k_ref[0, pl.ds(pl.multiple_of(kb0 * tb, tb), tk), :], q)
        if masked:
            key = kb0 * tb + lax.broadcasted_iota(jnp.int32, (tk, tq), 0)
            qry = qi * tq + lax.broadcasted_iota(jnp.int32, (tk, tq), 1)
            z = jnp.where((key <= qry) & (key >= jnp.maximum(step * tk, FIRST_VALID)), z, MASK_VALUE)
        z_ref[step & 1] = z
        return _reduce_rows(z, jnp.maximum)

    def back(step, z_max, state):
        z = z_ref[step & 1]
        m_prev, l_prev, acc = state
        kb0 = first_block(step)
        m_new = jnp.maximum(m_prev, z_max)
        p = jnp.exp2(z - m_new)
        alpha = jnp.exp2(m_prev - m_new)
        l_new = alpha * l_prev + _reduce_rows(p, jnp.add)
        vt = jnp.concatenate([vt_ref[0, 0, kb0 + i] for i in range(per_step)], axis=1)
        return m_new, l_new, alpha * acc + _dot(vt, p.astype(BF16))

    init = (jnp.full((1, tq), MASK_VALUE, F32), jnp.zeros((1, tq), F32), jnp.zeros((MLA_V_DIM, tq), F32))
    top = lax.shift_right_logical(qi * tq + (tq - 1), int(math.log2(tk)))
    lowest = lax.shift_right_logical(qi * tq, int(math.log2(tk)))
    _, l, acc = _walk_key_steps(top, lowest, front, back, init)
    o_ref[0] = (acc / l).T.astype(o_ref.dtype)


def _mla_attn(q, k, vt):
    b, lp, _ = q.shape
    tb = ATT_BLOCK
    tq = tb * _blocks_per_step(lp, MLA_Q_BLOCKS)
    assert lp >= MLA_KEY_BLOCKS * tb and FIRST_VALID < MLA_KEY_BLOCKS * tb
    return pl.pallas_call(
        _mla_attn_body,
        grid=(b, MLA_HEADS, lp // tq),
        in_specs=[pl.BlockSpec((1, tq, MLA_QK_DIM), lambda bi, h, i: (bi, i, h)),
                  pl.BlockSpec((1, lp, MLA_QK_DIM), lambda bi, h, i: (bi, 0, h)),
                  pl.BlockSpec((1, 1, lp // tb, MLA_V_DIM, tb), lambda bi, h, i: (bi, h, 0, 0, 0))],
        out_specs=pl.BlockSpec((1, tq, MLA_V_DIM), lambda bi, h, i: (bi, i, h)),
        out_shape=jax.ShapeDtypeStruct((b, lp, MLA_HEADS * MLA_V_DIM), BF16),
        scratch_shapes=[pltpu.VMEM((2, MLA_KEY_BLOCKS * tb, tq), F32)],
        compiler_params=_params("parallel", "parallel", "arbitrary"),
        name="mla_attn",
    )(q, k, vt)


def _rotate_half_cols(w):
    half = MLA_ROPE_DIM // 2
    return jnp.concatenate([-w[..., half:], w[..., :half]], axis=-1)


def _mla_mixer(h2, g_pre, g_post, w_in, q_norm, kv_norm, w_uq, w_ukv, w_out, b, lp, tm):
    rank = MLA_Q_RANK + MLA_KV_RANK
    w_kr = w_in[:, rank:]
    w_in_x = jnp.concatenate([w_in[:, :rank], w_kr, _rotate_half_cols(w_kr)], axis=1).astype(BF16)
    wq = w_uq.reshape(MLA_Q_RANK, MLA_HEADS, MLA_NOPE_DIM + MLA_ROPE_DIM)
    wq_rope = wq[..., MLA_NOPE_DIM:]
    w_uq_x = jnp.concatenate([wq[..., :MLA_NOPE_DIM], wq_rope, _rotate_half_cols(wq_rope)], axis=-1)
    w_uq_x = w_uq_x.reshape(MLA_Q_RANK, MLA_HEADS * MLA_QK_DIM).astype(BF16)
    wkv = w_ukv.reshape(MLA_KV_RANK, MLA_HEADS, MLA_NOPE_DIM + MLA_V_DIM)
    w_ukv_x = jnp.concatenate([wkv[..., :MLA_NOPE_DIM].reshape(MLA_KV_RANK, -1),
                               wkv[..., MLA_NOPE_DIM:].reshape(MLA_KV_RANK, -1)], axis=1).astype(BF16)
    half = MLA_ROPE_DIM // 2
    inv_freq = ROPE_THETA ** (-jnp.arange(half, dtype=F32) / half)
    pos = jnp.arange(lp) - FIRST_VALID
    ang = pos.astype(F32)[:, None] * inv_freq[None, :]
    zeros = jnp.zeros((lp, LANES - MLA_ROPE_DIM), F32)
    cos = jnp.concatenate([jnp.cos(ang), jnp.cos(ang), zeros], axis=1)
    sin = jnp.concatenate([jnp.sin(ang), jnp.sin(ang), zeros], axis=1)
    q, k, v = _mla_proj(h2, g_pre, w_in_x, q_norm[None], kv_norm[None], w_uq_x, w_ukv_x, cos, sin, tm, lp)
    vt = jnp.transpose(v.reshape(b, lp // ATT_BLOCK, ATT_BLOCK, MLA_HEADS, MLA_V_DIM), (0, 3, 1, 4, 2))
    o = _mla_attn(q.reshape(b, lp, -1), k.reshape(b, lp, -1), vt)
    return _out_proj(o.reshape(b * lp, -1), h2, g_post, w_out.astype(BF16), tm, lp)


def _head_sum(x, bd_ref):
    return _dot(x.astype(BF16), bd_ref[...])


def _rw_proj_body(x_ref, xp_ref, g_ref, mix_ref, wrkv_ref, w0_ref, w1_ref, w2_ref, a0_ref, a1_ref, a2_ref,
                  g1_ref, g2_ref, kk_ref, ka_ref, rk_ref, bd_ref,
                  r_ref, lw_ref, k_ref, v_ref, na_ref, b_ref, gate_ref, bonus_ref, *, lp):
    tm = x_ref.shape[0]
    g = g_ref[...]
    u = _rms(x_ref[...], g)
    first = (pl.program_id(0) * tm) % lp == 0
    u_prev = jnp.where(first, 0.0, _rms(xp_ref[...], g)[7:8, :])
    shifted = jnp.where(lax.broadcasted_iota(jnp.int32, (tm, 1), 0) == 0, u_prev, pltpu.roll(u, 1, 0))
    xx = shifted - u
    mix = mix_ref[...]
    xs = lambda n: (u + xx * mix[n:n + 1, :]).astype(BF16)
    r = _dot(xs(0), wrkv_ref[0])
    k = _dot(xs(1), wrkv_ref[1])
    v = _dot(xs(2), wrkv_ref[2])
    wl = w0_ref[...] + _dot(jnp.tanh(_dot(xs(3), w1_ref[...])).astype(BF16), w2_ref[...])
    w_log = -(jnp.maximum(-wl, 0.0) + jnp.log(1.0 + jnp.exp(-jnp.abs(wl)))) - 0.5
    lw_ref[...] = -jnp.exp(w_log)
    a = jax.nn.sigmoid(a0_ref[...] + _dot(_dot(xs(4), a1_ref[...]).astype(BF16), a2_ref[...]))
    gate = _dot(jax.nn.sigmoid(_dot(xs(5), g1_ref[...])).astype(BF16), g2_ref[...])
    gate_ref[...] = gate.astype(gate_ref.dtype)
    kk = k * kk_ref[...]
    kk = kk / jnp.maximum(jnp.sqrt(_head_sum(kk * kk, bd_ref)), 1e-12)
    k2 = k * (1.0 + (a - 1.0) * ka_ref[...])
    r_ref[...] = r.astype(r_ref.dtype)
    k_ref[...] = k2.astype(k_ref.dtype)
    v_ref[...] = v.astype(v_ref.dtype)
    na_ref[...] = (-kk).astype(na_ref.dtype)
    b_ref[...] = (kk * a).astype(b_ref.dtype)
    bonus_ref[...] = _head_sum(r * k2 * rk_ref[...], bd_ref) * v


def _rw_proj(h2, g_pre, mix, w_rkv, w0, w1, w2, a0, a1, a2, g1, g2, k_k, k_a, r_k, bd, tm, lp):
    n, d = h2.shape
    row = pl.BlockSpec((tm, d), lambda i: (i, 0))
    prev = pl.BlockSpec((8, d), lambda i: (jnp.maximum(i * (tm // 8) - 1, 0), 0))
    vec = _resident((1, d))
    outs = [jax.ShapeDtypeStruct((n, d), dt) for dt in (BF16, F32, BF16, BF16, BF16, BF16, BF16, F32)]
    return pl.pallas_call(
        functools.partial(_rw_proj_body, lp=lp),
        grid=(n // tm,),
        in_specs=[row, prev, vec, _resident(mix.shape), _resident(w_rkv.shape), vec, _resident(w1.shape),
                  _resident(w2.shape), vec, _resident(a1.shape), _resident(a2.shape), _resident(g1.shape),
                  _resident(g2.shape), vec, vec, vec, _resident(bd.shape)],
        out_specs=[row] * 8,
        out_shape=outs,
        compiler_params=_params("parallel"),
        name="rw_proj",
    )(h2, h2, g_pre, mix, w_rkv, w0, w1, w2, a0, a1, a2, g1, g2, k_k, k_a, r_k, bd)


def _rw_scan_body(r_ref, lw_ref, k_ref, v_ref, a_ref, b_ref, y_ref, s_ref):
    c = RW_CHUNK
    gw = RW_GROUP
    heads = gw // c

    @pl.when(pl.program_id(2) == 0)
    def _():
        s_ref[...] = jnp.zeros_like(s_ref)

    ri = lax.broadcasted_iota(jnp.int32, (gw, gw), 0)
    ci = lax.broadcasted_iota(jnp.int32, (gw, gw), 1)
    shift = int(math.log2(c))
    block_mask = (ri >> shift) == (ci >> shift)
    t_idx = lax.broadcasted_iota(jnp.int32, (c, gw), 0)
    s_idx = lax.broadcasted_iota(jnp.int32, (c, gw), 1) & (c - 1)
    strict = s_idx < t_idx
    incl = s_idx <= t_idx
    eye = (s_idx == t_idx).astype(F32)
    tri = (lax.broadcasted_iota(jnp.int32, (c, c), 1) <= lax.broadcasted_iota(jnp.int32, (c, c), 0)).astype(BF16)

    def bd(x):
        return jnp.where(block_mask, jnp.concatenate([x.astype(BF16)] * heads, axis=0), 0.0)

    def lower(x, mask):
        return jnp.where(mask, x, 0.0).astype(BF16)

    n_chunks = r_ref.shape[1] // c
    pre = []
    for j in range(n_chunks):
        rows = slice(j * c, (j + 1) * c)
        r, lw, k, v, a, b = (ref[0, rows, :] for ref in (r_ref, lw_ref, k_ref, v_ref, a_ref, b_ref))
        cum = _cumsum_rows(lw, tri)
        total = cum[c - 1:c, :]
        lhs = jnp.concatenate([a * jnp.exp(cum - lw), r * jnp.exp(cum)], axis=0).astype(BF16)
        g_inv = jnp.exp(-cum)
        a_b = _dot_nt(lhs, bd(b * g_inv))
        a_k = _dot_nt(lhs, bd(k * g_inv))
        g_out = jnp.exp(total - cum)
        rhs_out = jnp.concatenate([b * g_out, k * g_out], axis=0).astype(BF16)
        pre.append(dict(lhs=lhs, v=v, total=total, rhs_out=rhs_out, n=jnp.where(strict, a_b[:c], 0.0),
                        a_ak=lower(a_k[:c], strict), a_rb=lower(a_b[c:], incl), a_rk=lower(a_k[c:], incl)))
    t_mats = [eye + p["n"] for p in pre]
    pws = [p["n"] for p in pre]
    for _ in range(int(math.log2(c)) - 1):
        pws = [_dot(pw.astype(BF16), bd(pw)) for pw in pws]
        t_mats = [t + _dot(t.astype(BF16), bd(pw)) for t, pw in zip(t_mats, pws)]
    affine = []
    for j in range(n_chunks):
        p = pre[j]
        t16 = t_mats[j].astype(BF16)
        bd_v = bd(p["v"])
        w1 = _dot(t16, bd(p["lhs"][:c]))
        u0 = _dot(t16, bd(_dot(p["a_ak"], bd_v)))
        r2 = (p["lhs"][c:].astype(F32) + _dot(p["a_rb"], bd(w1))).astype(BF16)
        y0 = _dot(p["a_rb"], bd(u0)) + _dot(p["a_rk"], bd_v)
        p_mat = jnp.where(block_mask, _dot_tn(w1.astype(BF16), p["rhs_out"][:c]), 0.0).astype(BF16)
        n0 = jnp.where(block_mask, _dot_tn(jnp.concatenate([u0, p["v"]], axis=0).astype(BF16), p["rhs_out"]), 0.0)
        affine.append((r2, y0, p_mat, n0, jnp.exp(p["total"])))
    s_t = s_ref[...]
    for j in range(n_chunks):
        r2, y0, p_mat, n0, decay = affine[j]
        s16 = s_t.astype(BF16)
        y_ref[0, j * c:(j + 1) * c, :] = _dot_nt(r2, s16) + y0
        s_t = s_t * decay + _dot(s16, p_mat) + n0
    s_ref[...] = s_t


def _cumsum_rows(x, tri):
    hi, lo = _split(x)
    return _dot(tri, hi) + _dot(tri, lo)


def _rw_scan(r, lw, k, v, a, b):
    bsz, lp, d = r.shape
    rows = RW_CHUNK * next(n for n in RW_CHUNKS_PER_STEP if lp % (RW_CHUNK * n) == 0)
    spec = pl.BlockSpec((1, rows, RW_GROUP), lambda bi, g, i: (bi, i, g))
    return pl.pallas_call(
        _rw_scan_body,
        grid=(bsz, d // RW_GROUP, lp // rows),
        in_specs=[spec] * 6,
        out_specs=spec,
        out_shape=jax.ShapeDtypeStruct((bsz, lp, d), F32),
        scratch_shapes=[pltpu.VMEM((RW_GROUP, RW_GROUP), F32)],
        compiler_params=_params("parallel", "parallel", "arbitrary"),
        name="rw_scan",
    )(r, lw, k, v, a, b)


def _rw_out_body(y_ref, bonus_ref, gate_ref, h_ref, lnw_ref, lnb_ref, g_ref, w_ref, bd_ref, out_ref, *, lp):
    tm = h_ref.shape[0]
    y = y_ref[...]
    mu = _head_sum(y, bd_ref) * (1.0 / RW_HEAD_DIM)
    dlt = y - mu
    var = _head_sum(dlt * dlt, bd_ref) * (1.0 / RW_HEAD_DIM)
    yn = dlt * lax.rsqrt(var + RW_LN_EPS) * lnw_ref[...] + lnb_ref[...]
    z = ((yn + bonus_ref[...]) * gate_ref[...]).astype(BF16)
    m = _dot(z, w_ref[...])
    upd = jnp.where(_row_valid(tm, lp), _rms(m, g_ref[...]), 0.0)
    out_ref[...] = h_ref[...] + upd


def _rw_out(y2, bonus, gate, h2, ln_w, ln_b, g_post, w_out, bd, tm, lp):
    n, d = h2.shape
    row = pl.BlockSpec((tm, d), lambda i: (i, 0))
    vec = _resident((1, d))
    return pl.pallas_call(
        functools.partial(_rw_out_body, lp=lp),
        grid=(n // tm,),
        in_specs=[row, row, row, row, vec, vec, vec, _resident(w_out.shape), _resident(bd.shape)],
        out_specs=row,
        out_shape=jax.ShapeDtypeStruct((n, d), F32),
        compiler_params=_params("parallel"),
        name="rw_out",
    )(y2, bonus, gate, h2, ln_w, ln_b, g_post, w_out, bd)


def _pad_cols(w, width):
    return jnp.pad(w, ((0, 0), (0, width - w.shape[1])))


def _pad_rows(w, height):
    return jnp.pad(w, ((0, height - w.shape[0]), (0, 0)))


def _rwkv7_mixer(h2, g_pre, g_post, mix, w_in, w0, w1, w2, a0, a1, a2, g1, g2, k_k, k_a, r_k, ln_w, ln_b, w_out,
                 b, lp, tm):
    d = h2.shape[1]
    idx = jnp.arange(d) // RW_HEAD_DIM
    bd = (idx[:, None] == idx[None, :]).astype(BF16)
    lora = lambda w_a, w_b: (_pad_cols(w_a, -(-w_a.shape[1] // LANES) * LANES).astype(BF16),
                             _pad_rows(w_b, -(-w_b.shape[0] // LANES) * LANES).astype(BF16))
    w1p, w2p = lora(w1, w2)
    a1p, a2p = lora(a1, a2)
    g1p, g2p = lora(g1, g2)
    outs = _rw_proj(h2, g_pre, mix, w_in.astype(BF16), w0[None], w1p, w2p, a0[None], a1p, a2p, g1p, g2p,
                    k_k[None], k_a[None], r_k.reshape(1, d), bd, RW_PROJ_ROWS, lp)
    r, lw, k, v, na, bb, gate, bonus = outs
    y = _rw_scan(*(t.reshape(b, lp, d) for t in (r, lw, k, v, na, bb)))
    return _rw_out(y.reshape(b * lp, d), bonus, gate, h2, ln_w[None], ln_b[None], g_post, w_out.astype(BF16), bd, tm, lp)


def kernel(x, meta_tokens, norm_gains, ffn_w_in, ffn_w_out, sb_w_in, sb_w_out, mla_w_in, mla_q_norm, mla_kv_norm, mla_w_uq, mla_w_ukv, mla_w_out, rw_mix, rw_w_in, rw_w0, rw_w1, rw_w2, rw_a0, rw_a1, rw_a2, rw_g1, rw_g2, rw_k_k, rw_k_a, rw_r_k, rw_ln_w, rw_ln_b, rw_w_out):
    b, seq, d = x.shape
    assert d == D_MODEL and seq % ATT_BLOCK == 0
    lp = PAD + seq
    tm = _row_tile(lp)
    meta = jnp.broadcast_to(meta_tokens.astype(x.dtype)[None], (b, N_META, d))
    h = jnp.concatenate([jnp.zeros((b, FIRST_VALID, d), x.dtype), meta, x], axis=1)
    h2 = h.reshape(b * lp, d)
    depth = norm_gains.shape[0]
    for i in range(depth):
        g = norm_gains[i][:, :, None, :]
        h2 = _ffn(h2, g[0, 0], g[0, 1], ffn_w_in[i, 0].astype(BF16), ffn_w_out[i, 0].astype(BF16), tm)
        kind, slot = i % 3, i // 3
        if kind == 0:
            h2 = _stick_breaking_mixer(h2, g[1, 0], g[1, 1], sb_w_in[slot], sb_w_out[slot], b, lp, tm)
        elif kind == 1:
            h2 = _mla_mixer(h2, g[1, 0], g[1, 1], mla_w_in[slot], mla_q_norm[slot], mla_kv_norm[slot],
                            mla_w_uq[slot], mla_w_ukv[slot], mla_w_out[slot], b, lp, tm)
        else:
            h2 = _rwkv7_mixer(h2, g[1, 0], g[1, 1], rw_mix[slot], rw_w_in[slot], rw_w0[slot], rw_w1[slot],
                              rw_w2[slot], rw_a0[slot], rw_a1[slot], rw_a2[slot], rw_g1[slot], rw_g2[slot],
                              rw_k_k[slot], rw_k_a[slot], rw_r_k[slot], rw_ln_w[slot], rw_ln_b[slot],
                              rw_w_out[slot], b, lp, tm)
        h2 = _ffn(h2, g[2, 0], g[2, 1], ffn_w_in[i, 1].astype(BF16), ffn_w_out[i, 1].astype(BF16), tm)
    return h2.reshape(b, lp, d)[:, PAD:]
```

```python
import functools
import math

import jax
import jax.numpy as jnp
from jax import lax
from jax.experimental import pallas as pl
from jax.experimental.pallas import tpu as pltpu

F32 = jnp.float32
BF16 = jnp.bfloat16

D_MODEL = 1024
N_META = 16
RMS_EPS = 1e-6
SB_HEADS = 8
SB_HEAD_DIM = D_MODEL // SB_HEADS
SB_SCALE = 1.0 / math.sqrt(SB_HEAD_DIM)
MLA_HEADS = 8
MLA_NOPE_DIM = 128
MLA_ROPE_DIM = 64
MLA_V_DIM = 128
MLA_Q_RANK = 3 * D_MODEL // 8
MLA_KV_RANK = D_MODEL // 4
MLA_SCALE = 1.0 / math.sqrt(MLA_NOPE_DIM + MLA_ROPE_DIM)
ROPE_THETA = 10000.0
RW_HEAD_DIM = 64
RW_LN_EPS = 64e-5
D_FF = ((8 * D_MODEL // 3 + 127) // 128) * 128

LANES = 128
MXU_DIM = 256
PAD = 256
FIRST_VALID = PAD - N_META
ATT_BLOCK = MXU_DIM
FFN_CHUNK = MXU_DIM
RW_CHUNK = RW_HEAD_DIM
RW_GROUP = MXU_DIM
RW_CHUNKS_PER_STEP = (10, 5, 4, 2, 1)
RW_PROJ_ROWS = 256
MASK_VALUE = -1e30
SB_UNDERFLOW_LOG = 110.0
VMEM_LIMIT = 56 * 1024 * 1024


def _row_tile(lp):
    for t in (640, 512, 256, 128):
        if lp % t == 0:
            return t
    raise ValueError(f"unsupported padded length {lp}")


def _params(*sem, flags=None):
    return pltpu.CompilerParams(dimension_semantics=sem, vmem_limit_bytes=VMEM_LIMIT, flags=flags)


def _resident(shape):
    nd = len(shape)
    return pl.BlockSpec(shape, lambda *_: (0,) * nd, pipeline_mode=pl.Buffered(1))


def _rms(x, g):
    return x * lax.rsqrt(jnp.mean(x * x, axis=-1, keepdims=True) + RMS_EPS) * g


def _dot(a, b):
    return jnp.dot(a, b, preferred_element_type=F32)


def _dot_nt(a, b):
    return lax.dot_general(a, b, (((1,), (1,)), ((), ())), preferred_element_type=F32)


def _dot_tn(a, b):
    return lax.dot_general(a, b, (((0,), (0,)), ((), ())), preferred_element_type=F32)


def _split(x):
    hi = x.astype(BF16)
    lo = (x - hi.astype(F32)).astype(BF16)
    return hi, lo


def _row_valid(tile_rows, lp):
    pos0 = (pl.program_id(0) * tile_rows) % lp
    pos = pos0 + lax.broadcasted_iota(jnp.int32, (tile_rows, 1), 0)
    return pos >= FIRST_VALID


def _ffn_body(x_ref, gpre_ref, gpost_ref, win_ref, wout_ref, o_ref):
    x = x_ref[...]
    xn = _rms(x, gpre_ref[...]).astype(BF16)
    acc = jnp.zeros(x.shape, F32)
    for c in range(D_FF // FFN_CHUNK):
        lo = c * FFN_CHUNK
        gate = _dot(xn, win_ref[:, lo:lo + FFN_CHUNK])
        up = _dot(xn, win_ref[:, D_FF + lo:D_FF + lo + FFN_CHUNK])
        act = (gate * jax.nn.sigmoid(gate) * up).astype(BF16)
        acc = acc + _dot(act, wout_ref[lo:lo + FFN_CHUNK, :])
    o_ref[...] = x + 0.5 * _rms(acc, gpost_ref[...])


def _ffn(h2, g_pre, g_post, w_in, w_out, tm):
    n, d = h2.shape
    row = pl.BlockSpec((tm, d), lambda i: (i, 0))
    return pl.pallas_call(
        _ffn_body,
        grid=(n // tm,),
        in_specs=[row, _resident((1, d)), _resident((1, d)), _resident(w_in.shape), _resident(w_out.shape)],
        out_specs=row,
        out_shape=jax.ShapeDtypeStruct((n, d), F32),
        compiler_params=_params("parallel"),
        name="ffn",
    )(h2, g_pre, g_post, w_in, w_out)


def _out_proj_body(o_ref, h_ref, g_ref, w_ref, out_ref, *, lp):
    tm = h_ref.shape[0]
    m = _dot(o_ref[...], w_ref[...])
    upd = jnp.where(_row_valid(tm, lp), _rms(m, g_ref[...]), 0.0)
    out_ref[...] = h_ref[...] + upd


def _out_proj(o2, h2, g_post, w_out, tm, lp):
    n, d = h2.shape
    return pl.pallas_call(
        functools.partial(_out_proj_body, lp=lp),
        grid=(n // tm,),
        in_specs=[pl.BlockSpec((tm, o2.shape[1]), lambda i: (i, 0)), pl.BlockSpec((tm, d), lambda i: (i, 0)),
                  _resident((1, d)), _resident(w_out.shape)],
        out_specs=pl.BlockSpec((tm, d), lambda i: (i, 0)),
        out_shape=jax.ShapeDtypeStruct((n, d), F32),
        compiler_params=_params("parallel"),
        name="out_proj",
    )(o2, h2, g_post, w_out)


def _sb_proj_body(x_ref, g_ref, w_ref, qkv_ref):
    u = _rms(x_ref[...], g_ref[...]).astype(BF16)
    qkv = _dot(u, w_ref[...])
    d = x_ref.shape[1]
    qkv_ref[:, :d] = (qkv[:, :d] * SB_SCALE).astype(BF16)
    qkv_ref[:, d:] = qkv[:, d:].astype(BF16)


def _sb_proj(h2, g_pre, w_in, tm):
    n, d = h2.shape
    return pl.pallas_call(
        _sb_proj_body,
        grid=(n // tm,),
        in_specs=[pl.BlockSpec((tm, d), lambda i: (i, 0)), _resident((1, d)), _resident(w_in.shape)],
        out_specs=pl.BlockSpec((tm, w_in.shape[1]), lambda i: (i, 0)),
        out_shape=jax.ShapeDtypeStruct((n, w_in.shape[1]), BF16),
        compiler_params=_params("parallel"),
        name="sb_proj",
    )(h2, g_pre, w_in)


def _walk_key_steps(top, lowest_masked, front, back, state):
    def trip(masked):
        def body(_, carried):
            step, fr, st = carried
            st = back(step, fr, st)
            return step - 1, front(step - 1, masked), st
        return body

    carried = (top, front(top, True), state)
    carried = lax.fori_loop(0, top - lowest_masked, trip(True), carried)
    _, fr, st = lax.fori_loop(0, lowest_masked, trip(False), carried)
    return back(0, fr, st)


def _reduce_rows(x, op, group=64):
    part = functools.reduce(op, [x[i:i + group] for i in range(0, x.shape[0], group)])
    return jnp.max(part, axis=0, keepdims=True) if op is jnp.maximum else jnp.sum(part, axis=0, keepdims=True)


def _lane_tile(x, reps):
    return x if reps == 1 else jnp.concatenate([x] * reps, axis=1)


def _sb_attn_body(q_ref, k_ref, v_ref, tri_ref, o_ref):
    tb = ATT_BLOCK
    n = q_ref.shape[1] // tb
    qi = pl.program_id(2)
    tri = tri_ref[...]
    sign = jnp.int32(-2 ** 31)
    last = FIRST_VALID // tb

    def step(c, kb, masked, carry, acc):
        k0 = pl.multiple_of(jnp.maximum(kb, 0) * tb, tb)
        k = k_ref[0, pl.ds(k0, tb), :]
        v = v_ref[0, pl.ds(k0, tb), :]
        z = _dot_nt(q_ref[0, c * tb:(c + 1) * tb, :], k)
        neg_abs = lax.bitcast_convert_type(lax.bitcast_convert_type(z, jnp.int32) | sign, F32)
        sp = jnp.maximum(z, 0.0) + jnp.log(1.0 + jnp.exp(neg_abs))
        if masked:
            row = (qi * n + c) * tb + lax.broadcasted_iota(jnp.int32, (tb, tb), 0)
            col = k0 + lax.broadcasted_iota(jnp.int32, (tb, tb), 1)
            valid = (col < row) & (col >= FIRST_VALID) & (kb >= last)
            sp = jnp.where(valid, sp, 0.0)
        within = _dot(sp.astype(BF16), tri)
        p = jnp.exp(z - within - _lane_tile(carry, tb // LANES))
        if masked:
            p = jnp.where(valid, p, 0.0)
        return carry + jnp.broadcast_to(within[:, 0:1], carry.shape), acc + _dot(p.astype(BF16), v)

    def round_(j, masked, state):
        return tuple(step(c, qi * n + c - j, masked, *state[c]) for c in range(n))

    def live(state):
        carry = functools.reduce(jnp.minimum, [s[0] for s in state])
        return jnp.min(carry) < SB_UNDERFLOW_LOG

    zero = (jnp.zeros((tb, LANES), F32), jnp.zeros((tb, SB_HEAD_DIM), F32))
    state = round_(0, True, (zero,) * n)
    j, state = lax.while_loop(lambda c: (qi * n - c[0] > last) & live(c[1]),
                              lambda c: (c[0] + 1, round_(c[0], False, c[1])), (1, state))
    j, state = lax.while_loop(lambda c: (qi * n + (n - 1) - c[0] >= last) & live(c[1]),
                              lambda c: (c[0] + 1, round_(c[0], True, c[1])), (j, state))
    for c in range(n):
        o_ref[0, c * tb:(c + 1) * tb, :] = state[c][1].astype(o_ref.dtype)


def _sb_chains(lp):
    return next(n for n in (5, 4, 3, 2, 1) if (lp // ATT_BLOCK) % n == 0)


def _sb_attn(qkv, tri):
    b, lp, _ = qkv.shape
    tb = ATT_BLOCK * _sb_chains(lp)
    hd = SB_HEAD_DIM
    return pl.pallas_call(
        _sb_attn_body,
        grid=(b, SB_HEADS, lp // tb),
        in_specs=[pl.BlockSpec((1, tb, hd), lambda bi, h, i: (bi, i, h)),
                  pl.BlockSpec((1, lp, hd), lambda bi, h, i: (bi, 0, SB_HEADS + h)),
                  pl.BlockSpec((1, lp, hd), lambda bi, h, i: (bi, 0, 2 * SB_HEADS + h)),
                  _resident(tri.shape)],
        out_specs=pl.BlockSpec((1, tb, hd), lambda bi, h, i: (bi, i, h)),
        out_shape=jax.ShapeDtypeStruct((b, lp, SB_HEADS * hd), BF16),
        compiler_params=_params("parallel", "parallel", "arbitrary"),
        name="sb_attn",
    )(qkv, qkv, qkv, tri)


def _stick_breaking_mixer(h2, g_pre, g_post, w_in, w_out, b, lp, tm):
    qkv = _sb_proj(h2, g_pre, w_in.astype(BF16), tm)
    idx = jnp.arange(ATT_BLOCK)
    tri = (idx[:, None] >= idx[None, :]).astype(BF16)
    o = _sb_attn(qkv.reshape(b, lp, -1), tri)
    return _out_proj(o.reshape(b * lp, -1), h2, g_post, w_out.astype(BF16), tm, lp)


MLA_QK_DIM = MXU_DIM
MLA_Q_SCALE = MLA_SCALE * math.log2(math.e)
MLA_KEY_BLOCKS = 8

def _rope_tile(t, cos, sin):
    return t * cos + pltpu.roll(t, MLA_ROPE_DIM, 1) * sin


def _mla_proj_body(x_ref, g_ref, win_ref, qn_ref, kvn_ref, wuq_ref, wukv_ref, cos_ref, sin_ref, q_ref, k_ref, v_ref,
                   *, lp):
    u = _rms(x_ref[...], g_ref[...]).astype(BF16)
    c = _dot(u, win_ref[...])
    cq = _rms(c[:, :MLA_Q_RANK], qn_ref[...]).astype(BF16)
    ckv = _rms(c[:, MLA_Q_RANK:MLA_Q_RANK + MLA_KV_RANK], kvn_ref[...]).astype(BF16)
    cos = cos_ref[...]
    sin = sin_ref[...]
    tm = x_ref.shape[0]
    bias_lane = lax.broadcasted_iota(jnp.int32, (tm, LANES), 1) == MLA_ROPE_DIM
    q_bias = jnp.where(bias_lane, 1.0, 0.0)
    k_bias = jnp.where(bias_lane & jnp.logical_not(_row_valid(tm, lp)), MASK_VALUE, 0.0)
    k_rope = (_rope_tile(c[:, MLA_Q_RANK + MLA_KV_RANK:], cos, sin) + k_bias).astype(BF16)
    q = _dot(cq, wuq_ref[...])
    kv = _dot(ckv, wukv_ref[...])
    for h in range(MLA_HEADS):
        lo = h * MLA_QK_DIM
        q_ref[:, lo:lo + LANES] = (q[:, lo:lo + LANES] * MLA_Q_SCALE).astype(BF16)
        q_rope = _rope_tile(q[:, lo + LANES:lo + 2 * LANES], cos, sin)
        q_ref[:, lo + LANES:lo + 2 * LANES] = (q_rope * MLA_Q_SCALE + q_bias).astype(BF16)
        k_ref[:, lo:lo + LANES] = kv[:, h * LANES:(h + 1) * LANES].astype(BF16)
        k_ref[:, lo + LANES:lo + 2 * LANES] = k_rope
    v_ref[...] = kv[:, MLA_HEADS * MLA_NOPE_DIM:].astype(BF16)


def _mla_proj(h2, g_pre, w_in, q_norm, kv_norm, w_uq, w_ukv, cos, sin, tm, lp):
    n, d = h2.shape
    tiles_per_seq = lp // tm
    row = lambda w: pl.BlockSpec((tm, w), lambda i: (i, 0))
    table = pl.BlockSpec((tm, LANES), lambda i: (i % tiles_per_seq, 0))
    qk_w = MLA_HEADS * MLA_QK_DIM
    v_w = MLA_HEADS * MLA_V_DIM
    return pl.pallas_call(
        functools.partial(_mla_proj_body, lp=lp),
        grid=(n // tm,),
        in_specs=[row(d), _resident((1, d)), _resident(w_in.shape), _resident(q_norm.shape), _resident(kv_norm.shape),
                  _resident(w_uq.shape), _resident(w_ukv.shape), table, table],
        out_specs=[row(qk_w), row(qk_w), row(v_w)],
        out_shape=[jax.ShapeDtypeStruct((n, qk_w), BF16), jax.ShapeDtypeStruct((n, qk_w), BF16),
                   jax.ShapeDtypeStruct((n, v_w), BF16)],
        compiler_params=_params("parallel"),
        name="mla_proj",
    )(h2, g_pre, w_in, q_norm, kv_norm, w_uq, w_ukv, cos, sin)


def _mla_attn_body(q_ref, k_ref, vt_ref, o_ref, z_ref):
    tb = ATT_BLOCK
    per_step = MLA_KEY_BLOCKS
    tk = per_step * tb
    tq = q_ref.shape[1]
    lp = k_ref.shape[1]
    qi = pl.program_id(2)
    q = q_ref[0]

    def first_block(step):
        return jnp.minimum(step * per_step, lp // tb - per_step)

    def front(step, masked):
        kb0 = first_block(step)
        z = _dot_nt(k_ref[0, pl.ds(pl.multiple_of(kb0 * tb, tb), tk), :], q)
        if masked:
            key = kb0 * tb + lax.broadcasted_iota(jnp.int32, (tk, tq), 0)
            qry = qi * tq + lax.broadcasted_iota(jnp.int32, (tk, tq), 1)
            z = jnp.where((key <= qry) & (key >= jnp.maximum(step * tk, FIRST_VALID)), z, MASK_VALUE)
        z_ref[step & 1] = z
        return _reduce_rows(z, jnp.maximum)

    def back(step, z_max, state):
        z = z_ref[step & 1]
        m_prev, l_prev, acc = state
        kb0 = first_block(step)
        m_new = jnp.maximum(m_prev, z_max)
        p = jnp.exp2(z - m_new)
        alpha = jnp.exp2(m_prev - m_new)
        l_new = alpha * l_prev + _reduce_rows(p, jnp.add)
        vt = jnp.concatenate([vt_ref[0, 0, kb0 + i] for i in range(per_step)], axis=1)
        return m_new, l_new, alpha * acc + _dot(vt, p.astype(BF16))

    init = (jnp.full((1, tq), MASK_VALUE, F32), jnp.zeros((1, tq), F32), jnp.zeros((MLA_V_DIM, tq), F32))
    top = lax.shift_right_logical(qi * tq + (tq - 1), int(math.log2(tk)))
    lowest = lax.shift_right_logical(qi * tq, int(math.log2(tk)))
    _, l, acc = _walk_key_steps(top, lowest, front, back, init)
    o_ref[0] = (acc / l).T.astype(o_ref.dtype)


def _mla_attn(q, k, vt):
    b, lp, _ = q.shape
    tb = ATT_BLOCK
    tq = tb * _sb_chains(lp)
    assert lp >= MLA_KEY_BLOCKS * tb and FIRST_VALID < MLA_KEY_BLOCKS * tb
    return pl.pallas_call(
        _mla_attn_body,
        grid=(b, MLA_HEADS, lp // tq),
        in_specs=[pl.BlockSpec((1, tq, MLA_QK_DIM), lambda bi, h, i: (bi, i, h)),
                  pl.BlockSpec((1, lp, MLA_QK_DIM), lambda bi, h, i: (bi, 0, h)),
                  pl.BlockSpec((1, 1, lp // tb, MLA_V_DIM, tb), lambda bi, h, i: (bi, h, 0, 0, 0))],
        out_specs=pl.BlockSpec((1, tq, MLA_V_DIM), lambda bi, h, i: (bi, i, h)),
        out_shape=jax.ShapeDtypeStruct((b, lp, MLA_HEADS * MLA_V_DIM), BF16),
        scratch_shapes=[pltpu.VMEM((2, MLA_KEY_BLOCKS * tb, tq), F32)],
        compiler_params=_params("parallel", "parallel", "arbitrary"),
        name="mla_attn",
    )(q, k, vt)


def _rotate_half_cols(w):
    half = MLA_ROPE_DIM // 2
    return jnp.concatenate([-w[..., half:], w[..., :half]], axis=-1)


def _mla_mixer(h2, g_pre, g_post, w_in, q_norm, kv_norm, w_uq, w_ukv, w_out, b, lp, tm):
    rank = MLA_Q_RANK + MLA_KV_RANK
    w_kr = w_in[:, rank:]
    w_in_x = jnp.concatenate([w_in[:, :rank], w_kr, _rotate_half_cols(w_kr)], axis=1).astype(BF16)
    wq = w_uq.reshape(MLA_Q_RANK, MLA_HEADS, MLA_NOPE_DIM + MLA_ROPE_DIM)
    wq_rope = wq[..., MLA_NOPE_DIM:]
    w_uq_x = jnp.concatenate([wq[..., :MLA_NOPE_DIM], wq_rope, _rotate_half_cols(wq_rope)], axis=-1)
    w_uq_x = w_uq_x.reshape(MLA_Q_RANK, MLA_HEADS * MLA_QK_DIM).astype(BF16)
    wkv = w_ukv.reshape(MLA_KV_RANK, MLA_HEADS, MLA_NOPE_DIM + MLA_V_DIM)
    w_ukv_x = jnp.concatenate([wkv[..., :MLA_NOPE_DIM].reshape(MLA_KV_RANK, -1),
                               wkv[..., MLA_NOPE_DIM:].reshape(MLA_KV_RANK, -1)], axis=1).astype(BF16)
    half = MLA_ROPE_DIM // 2
    inv_freq = ROPE_THETA ** (-jnp.arange(half, dtype=F32) / half)
    pos = jnp.arange(lp) - FIRST_VALID
    ang = pos.astype(F32)[:, None] * inv_freq[None, :]
    zeros = jnp.zeros((lp, LANES - MLA_ROPE_DIM), F32)
    cos = jnp.concatenate([jnp.cos(ang), jnp.cos(ang), zeros], axis=1)
    sin = jnp.concatenate([jnp.sin(ang), jnp.sin(ang), zeros], axis=1)
    q, k, v = _mla_proj(h2, g_pre, w_in_x, q_norm[None], kv_norm[None], w_uq_x, w_ukv_x, cos, sin, tm, lp)
    vt = jnp.transpose(v.reshape(b, lp // ATT_BLOCK, ATT_BLOCK, MLA_HEADS, MLA_V_DIM), (0, 3, 1, 4, 2))
    o = _mla_attn(q.reshape(b, lp, -1), k.reshape(b, lp, -1), vt)
    return _out_proj(o.reshape(b * lp, -1), h2, g_post, w_out.astype(BF16), tm, lp)


def _head_sum(x, bd_ref):
    return _dot(x.astype(BF16), bd_ref[...])


def _rw_proj_body(x_ref, xp_ref, g_ref, mix_ref, wrkv_ref, w0_ref, w1_ref, w2_ref, a0_ref, a1_ref, a2_ref,
                  g1_ref, g2_ref, kk_ref, ka_ref, rk_ref, bd_ref,
                  r_ref, lw_ref, k_ref, v_ref, na_ref, b_ref, gate_ref, bonus_ref, *, lp):
    tm = x_ref.shape[0]
    g = g_ref[...]
    u = _rms(x_ref[...], g)
    first = (pl.program_id(0) * tm) % lp == 0
    u_prev = jnp.where(first, 0.0, _rms(xp_ref[...], g)[7:8, :])
    shifted = jnp.where(lax.broadcasted_iota(jnp.int32, (tm, 1), 0) == 0, u_prev, pltpu.roll(u, 1, 0))
    xx = shifted - u
    mix = mix_ref[...]
    xs = lambda n: (u + xx * mix[n:n + 1, :]).astype(BF16)
    r = _dot(xs(0), wrkv_ref[0])
    k = _dot(xs(1), wrkv_ref[1])
    v = _dot(xs(2), wrkv_ref[2])
    wl = w0_ref[...] + _dot(jnp.tanh(_dot(xs(3), w1_ref[...])).astype(BF16), w2_ref[...])
    w_log = -(jnp.maximum(-wl, 0.0) + jnp.log(1.0 + jnp.exp(-jnp.abs(wl)))) - 0.5
    lw_ref[...] = -jnp.exp(w_log)
    a = jax.nn.sigmoid(a0_ref[...] + _dot(_dot(xs(4), a1_ref[...]).astype(BF16), a2_ref[...]))
    gate = _dot(jax.nn.sigmoid(_dot(xs(5), g1_ref[...])).astype(BF16), g2_ref[...])
    gate_ref[...] = gate.astype(gate_ref.dtype)
    kk = k * kk_ref[...]
    kk = kk / jnp.maximum(jnp.sqrt(_head_sum(kk * kk, bd_ref)), 1e-12)
    k2 = k * (1.0 + (a - 1.0) * ka_ref[...])
    r_ref[...] = r.astype(r_ref.dtype)
    k_ref[...] = k2.astype(k_ref.dtype)
    v_ref[...] = v.astype(v_ref.dtype)
    na_ref[...] = (-kk).astype(na_ref.dtype)
    b_ref[...] = (kk * a).astype(b_ref.dtype)
    bonus_ref[...] = _head_sum(r * k2 * rk_ref[...], bd_ref) * v


def _rw_proj(h2, g_pre, mix, w_rkv, w0, w1, w2, a0, a1, a2, g1, g2, k_k, k_a, r_k, bd, tm, lp):
    n, d = h2.shape
    row = pl.BlockSpec((tm, d), lambda i: (i, 0))
    prev = pl.BlockSpec((8, d), lambda i: (jnp.maximum(i * (tm // 8) - 1, 0), 0))
    vec = _resident((1, d))
    outs = [jax.ShapeDtypeStruct((n, d), dt) for dt in (BF16, F32, BF16, BF16, BF16, BF16, BF16, F32)]
    return pl.pallas_call(
        functools.partial(_rw_proj_body, lp=lp),
        grid=(n // tm,),
        in_specs=[row, prev, vec, _resident(mix.shape), _resident(w_rkv.shape), vec, _resident(w1.shape),
                  _resident(w2.shape), vec, _resident(a1.shape), _resident(a2.shape), _resident(g1.shape),
                  _resident(g2.shape), vec, vec, vec, _resident(bd.shape)],
        out_specs=[row] * 8,
        out_shape=outs,
        compiler_params=_params("parallel"),
        name="rw_proj",
    )(h2, h2, g_pre, mix, w_rkv, w0, w1, w2, a0, a1, a2, g1, g2, k_k, k_a, r_k, bd)


def _rw_scan_body(r_ref, lw_ref, k_ref, v_ref, a_ref, b_ref, y_ref, s_ref):
    c = RW_CHUNK
    gw = RW_GROUP
    heads = gw // c

    @pl.when(pl.program_id(2) == 0)
    def _():
        s_ref[...] = jnp.zeros_like(s_ref)

    ri = lax.broadcasted_iota(jnp.int32, (gw, gw), 0)
    ci = lax.broadcasted_iota(jnp.int32, (gw, gw), 1)
    shift = int(math.log2(c))
    block_mask = (ri >> shift) == (ci >> shift)
    t_idx = lax.broadcasted_iota(jnp.int32, (c, gw), 0)
    s_idx = lax.broadcasted_iota(jnp.int32, (c, gw), 1) & (c - 1)
    strict = s_idx < t_idx
    incl = s_idx <= t_idx
    eye = (s_idx == t_idx).astype(F32)
    tri = (lax.broadcasted_iota(jnp.int32, (c, c), 1) <= lax.broadcasted_iota(jnp.int32, (c, c), 0)).astype(BF16)

    def bd(x):
        return jnp.where(block_mask, jnp.concatenate([x.astype(BF16)] * heads, axis=0), 0.0)

    def lower(x, mask):
        return jnp.where(mask, x, 0.0).astype(BF16)

    n_chunks = r_ref.shape[1] // c
    pre = []
    for j in range(n_chunks):
        rows = slice(j * c, (j + 1) * c)
        r, lw, k, v, a, b = (ref[0, rows, :] for ref in (r_ref, lw_ref, k_ref, v_ref, a_ref, b_ref))
        cum = _cumsum_rows(lw, tri)
        total = cum[c - 1:c, :]
        lhs = jnp.concatenate([a * jnp.exp(cum - lw), r * jnp.exp(cum)], axis=0).astype(BF16)
        g_inv = jnp.exp(-cum)
        a_b = _dot_nt(lhs, bd(b * g_inv))
        a_k = _dot_nt(lhs, bd(k * g_inv))
        g_out = jnp.exp(total - cum)
        rhs_out = jnp.concatenate([b * g_out, k * g_out], axis=0).astype(BF16)
        pre.append(dict(lhs=lhs, v=v, total=total, rhs_out=rhs_out, n=jnp.where(strict, a_b[:c], 0.0),
                        a_ak=lower(a_k[:c], strict), a_rb=lower(a_b[c:], incl), a_rk=lower(a_k[c:], incl)))
    t_mats = [eye + p["n"] for p in pre]
    pws = [p["n"] for p in pre]
    for _ in range(int(math.log2(c)) - 1):
        pws = [_dot(pw.astype(BF16), bd(pw)) for pw in pws]
        t_mats = [t + _dot(t.astype(BF16), bd(pw)) for t, pw in zip(t_mats, pws)]
    affine = []
    for j in range(n_chunks):
        p = pre[j]
        t16 = t_mats[j].astype(BF16)
        bd_v = bd(p["v"])
        w1 = _dot(t16, bd(p["lhs"][:c]))
        u0 = _dot(t16, bd(_dot(p["a_ak"], bd_v)))
        r2 = (p["lhs"][c:].astype(F32) + _dot(p["a_rb"], bd(w1))).astype(BF16)
        y0 = _dot(p["a_rb"], bd(u0)) + _dot(p["a_rk"], bd_v)
        p_mat = jnp.where(block_mask, _dot_tn(w1.astype(BF16), p["rhs_out"][:c]), 0.0).astype(BF16)
        n0 = jnp.where(block_mask, _dot_tn(jnp.concatenate([u0, p["v"]], axis=0).astype(BF16), p["rhs_out"]), 0.0)
        affine.append((r2, y0, p_mat, n0, jnp.exp(p["total"])))
    s_t = s_ref[...]
    for j in range(n_chunks):
        r2, y0, p_mat, n0, decay = affine[j]
        s16 = s_t.astype(BF16)
        y_ref[0, j * c:(j + 1) * c, :] = _dot_nt(r2, s16) + y0
        s_t = s_t * decay + _dot(s16, p_mat) + n0
    s_ref[...] = s_t


def _cumsum_rows(x, tri):
    hi, lo = _split(x)
    return _dot(tri, hi) + _dot(tri, lo)


def _rw_scan(r, lw, k, v, a, b):
    bsz, lp, d = r.shape
    rows = RW_CHUNK * next(n for n in RW_CHUNKS_PER_STEP if lp % (RW_CHUNK * n) == 0)
    spec = pl.BlockSpec((1, rows, RW_GROUP), lambda bi, g, i: (bi, i, g))
    return pl.pallas_call(
        _rw_scan_body,
        grid=(bsz, d // RW_GROUP, lp // rows),
        in_specs=[spec] * 6,
        out_specs=spec,
        out_shape=jax.ShapeDtypeStruct((bsz, lp, d), F32),
        scratch_shapes=[pltpu.VMEM((RW_GROUP, RW_GROUP), F32)],
        compiler_params=_params("parallel", "parallel", "arbitrary"),
        name="rw_scan",
    )(r, lw, k, v, a, b)


def _rw_out_body(y_ref, bonus_ref, gate_ref, h_ref, lnw_ref, lnb_ref, g_ref, w_ref, bd_ref, out_ref, *, lp):
    tm = h_ref.shape[0]
    y = y_ref[...]
    mu = _head_sum(y, bd_ref) * (1.0 / RW_HEAD_DIM)
    dlt = y - mu
    var = _head_sum(dlt * dlt, bd_ref) * (1.0 / RW_HEAD_DIM)
    yn = dlt * lax.rsqrt(var + RW_LN_EPS) * lnw_ref[...] + lnb_ref[...]
    z = ((yn + bonus_ref[...]) * gate_ref[...]).astype(BF16)
    m = _dot(z, w_ref[...])
    upd = jnp.where(_row_valid(tm, lp), _rms(m, g_ref[...]), 0.0)
    out_ref[...] = h_ref[...] + upd


def _rw_out(y2, bonus, gate, h2, ln_w, ln_b, g_post, w_out, bd, tm, lp):
    n, d = h2.shape
    row = pl.BlockSpec((tm, d), lambda i: (i, 0))
    vec = _resident((1, d))
    return pl.pallas_call(
        functools.partial(_rw_out_body, lp=lp),
        grid=(n // tm,),
        in_specs=[row, row, row, row, vec, vec, vec, _resident(w_out.shape), _resident(bd.shape)],
        out_specs=row,
        out_shape=jax.ShapeDtypeStruct((n, d), F32),
        compiler_params=_params("parallel"),
        name="rw_out",
    )(y2, bonus, gate, h2, ln_w, ln_b, g_post, w_out, bd)


def _pad_cols(w, width):
    return jnp.pad(w, ((0, 0), (0, width - w.shape[1])))


def _pad_rows(w, height):
    return jnp.pad(w, ((0, height - w.shape[0]), (0, 0)))


def _rwkv7_mixer(h2, g_pre, g_post, mix, w_in, w0, w1, w2, a0, a1, a2, g1, g2, k_k, k_a, r_k, ln_w, ln_b, w_out,
                 b, lp, tm):
    d = h2.shape[1]
    idx = jnp.arange(d) // RW_HEAD_DIM
    bd = (idx[:, None] == idx[None, :]).astype(BF16)
    lora = lambda w_a, w_b: (_pad_cols(w_a, -(-w_a.shape[1] // LANES) * LANES).astype(BF16),
                             _pad_rows(w_b, -(-w_b.shape[0] // LANES) * LANES).astype(BF16))
    w1p, w2p = lora(w1, w2)
    a1p, a2p = lora(a1, a2)
    g1p, g2p = lora(g1, g2)
    outs = _rw_proj(h2, g_pre, mix, w_in.astype(BF16), w0[None], w1p, w2p, a0[None], a1p, a2p, g1p, g2p,
                    k_k[None], k_a[None], r_k.reshape(1, d), bd, RW_PROJ_ROWS, lp)
    r, lw, k, v, na, bb, gate, bonus = outs
    y = _rw_scan(*(t.reshape(b, lp, d) for t in (r, lw, k, v, na, bb)))
    return _rw_out(y.reshape(b * lp, d), bonus, gate, h2, ln_w[None], ln_b[None], g_post, w_out.astype(BF16), bd, tm, lp)


def kernel(x, meta_tokens, norm_gains, ffn_w_in, ffn_w_out, sb_w_in, sb_w_out, mla_w_in, mla_q_norm, mla_kv_norm, mla_w_uq, mla_w_ukv, mla_w_out, rw_mix, rw_w_in, rw_w0, rw_w1, rw_w2, rw_a0, rw_a1, rw_a2, rw_g1, rw_g2, rw_k_k, rw_k_a, rw_r_k, rw_ln_w, rw_ln_b, rw_w_out):
    b, seq, d = x.shape
    assert d == D_MODEL and seq % ATT_BLOCK == 0
    lp = PAD + seq
    tm = _row_tile(lp)
    meta = jnp.broadcast_to(meta_tokens.astype(x.dtype)[None], (b, N_META, d))
    h = jnp.concatenate([jnp.zeros((b, FIRST_VALID, d), x.dtype), meta, x], axis=1)
    h2 = h.reshape(b * lp, d)
    depth = norm_gains.shape[0]
    for i in range(depth):
        g = norm_gains[i][:, :, None, :]
        h2 = _ffn(h2, g[0, 0], g[0, 1], ffn_w_in[i, 0].astype(BF16), ffn_w_out[i, 0].astype(BF16), tm)
        kind, slot = i % 3, i // 3
        if kind == 0:
            h2 = _stick_breaking_mixer(h2, g[1, 0], g[1, 1], sb_w_in[slot], sb_w_out[slot], b, lp, tm)
        elif kind == 1:
            h2 = _mla_mixer(h2, g[1, 0], g[1, 1], mla_w_in[slot], mla_q_norm[slot], mla_kv_norm[slot],
                            mla_w_uq[slot], mla_w_ukv[slot], mla_w_out[slot], b, lp, tm)
        else:
            h2 = _rwkv7_mixer(h2, g[1, 0], g[1, 1], rw_mix[slot], rw_w_in[slot], rw_w0[slot], rw_w1[slot],
                              rw_w2[slot], rw_a0[slot], rw_a1[slot], rw_a2[slot], rw_g1[slot], rw_g2[slot],
                              rw_k_k[slot], rw_k_a[slot], rw_r_k[slot], rw_ln_w[slot], rw_ln_b[slot],
                              rw_w_out[slot], b, lp, tm)
        h2 = _ffn(h2, g[2, 0], g[2, 1], ffn_w_in[i, 1].astype(BF16), ffn_w_out[i, 1].astype(BF16), tm)
    return h2.reshape(b, lp, d)[:, PAD:]
```
